```python
import jax, jax.numpy as jnp
from jax import lax
import numpy as np

D_MODEL = 1024
BATCH = 8
SEQ = 4096
DEPTH = 2

EPS = 1e-6
MLSTM_HEADS = 4
MLSTM_HEAD_DIM = 128
MLSTM_WIDTH = MLSTM_HEADS * MLSTM_HEAD_DIM
MLSTM_CHUNK = 128
CONV_GROUPS = 4
CONV_CH = D_MODEL // 2
CONV_K = 3
IN0_WIDTH = 4 * MLSTM_WIDTH + 2 * MLSTM_HEADS + 3 * CONV_CH
MIX0_WIDTH = MLSTM_WIDTH + CONV_CH
SB_HEADS = 16
SB_HEAD_DIM = D_MODEL // SB_HEADS
SB_BLOCK = 128
PEER_HEADS = 8
PEER_NKEYS = 128
PEER_EXPERTS = PEER_NKEYS * PEER_NKEYS
PEER_QDIM = 256
PEER_HALF = PEER_QDIM // 2
PEER_TOPK = 16
PEER_TOKEN_BLOCK = 128

kernel_name = "hybrid_mlstm_conv_stickbreak_peer"


def rmsnorm(x, g):
    xf = x.astype(jnp.float32)
    y = xf * lax.rsqrt(jnp.mean(xf * xf, axis=-1, keepdims=True) + EPS)
    return (y * g.astype(jnp.float32)).astype(x.dtype)


def ada_params(c, w, b):
    mod = jnp.einsum("bd,de->be", jax.nn.silu(c), w) + b
    return jnp.split(mod, 6, axis=-1)


def modulate(x, g, shift, scale):
    return rmsnorm(x, g) * (1 + scale[:, None, :]) + shift[:, None, :]


def mlstm_chunkwise(q, k, v, i_pre, f_pre):
    B, H, S, dh = q.shape
    L = MLSTM_CHUNK
    NC = S // L
    f32 = jnp.float32
    qc = (q.astype(f32) * dh ** -0.5).reshape(B, H, NC, L, dh)
    kc = k.astype(f32).reshape(B, H, NC, L, dh)
    vc = v.astype(f32).reshape(B, H, NC, L, dh)
    log_f = jax.nn.log_sigmoid(f_pre.astype(f32)).reshape(B, H, NC, L)
    log_i = i_pre.astype(f32).reshape(B, H, NC, L)
    b = jnp.cumsum(log_f, axis=-1)
    g_chunk = b[..., -1]
    a = g_chunk[..., None] - b + log_i

    def step(carry, inp):
        s_, n_, m_ = carry
        k_, v_, a_, g_ = inp
        m_new = jnp.maximum(g_ + m_, jnp.max(a_, axis=-1))
        decay = jnp.exp(g_ + m_ - m_new)
        w = jnp.exp(a_ - m_new[..., None])
        s_new = decay[..., None, None] * s_ + jnp.einsum("bhl,bhld,bhle->bhde", w, k_, v_)
        n_new = decay[..., None] * n_ + jnp.einsum("bhl,bhld->bhd", w, k_)
        return (s_new, n_new, m_new), (s_, n_, m_)

    init = (jnp.zeros((B, H, dh, dh), f32), jnp.zeros((B, H, dh), f32), jnp.zeros((B, H), f32))
    xs = (jnp.moveaxis(kc, 2, 0), jnp.moveaxis(vc, 2, 0), jnp.moveaxis(a, 2, 0), jnp.moveaxis(g_chunk, 2, 0))
    _, (s_prev, n_prev, m_prev) = lax.scan(step, init, xs)
    s_prev = jnp.moveaxis(s_prev, 0, 2)
    n_prev = jnp.moveaxis(n_prev, 0, 2)
    m_prev = jnp.moveaxis(m_prev, 0, 2)

    causal = jnp.tril(jnp.ones((L, L), dtype=bool))
    log_d = jnp.where(causal, b[..., :, None] - b[..., None, :] + log_i[..., None, :], -jnp.inf)
    inter_log = b + m_prev[..., None]
    m_t = jnp.maximum(inter_log, jnp.max(log_d, axis=-1))
    qk = jnp.einsum("bhcld,bhcsd->bhcls", qc, kc) * jnp.exp(log_d - m_t[..., None])
    inter_w = jnp.exp(inter_log - m_t)
    num = jnp.einsum("bhcls,bhcsd->bhcld", qk, vc) + inter_w[..., None] * jnp.einsum("bhcld,bhcde->bhcle", qc, s_prev)
    den = jnp.sum(qk, axis=-1) + inter_w * jnp.einsum("bhcld,bhcd->bhcl", qc, n_prev)
    h = num / jnp.maximum(jnp.abs(den), jnp.exp(-m_t))[..., None]
    return h.reshape(B, H, S, dh).astype(q.dtype)


def mlstm_conv_mixer(h, w_in, b_igate, b_fgate, conv_w, mlstm_norm, w_out):
    B, S, _ = h.shape
    W, Hn, dh = MLSTM_WIDTH, MLSTM_HEADS, MLSTM_HEAD_DIM
    proj = jnp.einsum("bsd,de->bse", h, w_in)
    cuts = [W, 2 * W, 3 * W, 4 * W, 4 * W + Hn, 4 * W + 2 * Hn,
            4 * W + 2 * Hn + CONV_CH, 4 * W + 2 * Hn + 2 * CONV_CH]
    q, k, v, o, gi, gf, cb, cc, ch = jnp.split(proj, cuts, axis=-1)

    def heads(t):
        return t.reshape(B, S, Hn, dh).transpose(0, 2, 1, 3)

    hm = mlstm_chunkwise(heads(q), heads(k), heads(v),
                         (gi + b_igate).transpose(0, 2, 1), (gf + b_fgate).transpose(0, 2, 1))
    hm = rmsnorm(hm.transpose(0, 2, 1, 3), mlstm_norm.reshape(Hn, dh))
    hm = hm.reshape(B, S, W) * jax.nn.sigmoid(o)
    u = cc * ch
    conv = lax.conv_general_dilated(u, conv_w[:, None, :], window_strides=(1,),
                                    padding=[(CONV_K - 1, 0)],
                                    dimension_numbers=("NWC", "WIO", "NWC"),
                                    feature_group_count=CONV_CH)
    hc = cb * conv
    return jnp.einsum("bse,ed->bsd", jnp.concatenate([hm, hc], axis=-1), w_out)


def stick_breaking_mixer(h, w_qkv, q_norm, k_norm, w_out):
    B, S, _ = h.shape
    f32 = jnp.float32
    qkv = jnp.einsum("bsd,de->bse", h, w_qkv).reshape(B, S, 3, SB_HEADS, SB_HEAD_DIM)
    q = rmsnorm(qkv[:, :, 0], q_norm).astype(f32).transpose(0, 2, 1, 3)
    k = rmsnorm(qkv[:, :, 1], k_norm).astype(f32).transpose(0, 2, 1, 3)
    v = qkv[:, :, 2].astype(f32).transpose(0, 2, 1, 3)
    scale = SB_HEAD_DIM ** -0.5
    outs = []
    for blk in range(S // SB_BLOCK):
        q0 = blk * SB_BLOCK
        kend = q0 + SB_BLOCK
        z = jnp.einsum("bhtd,bhsd->bhts", q[:, :, q0:kend], k[:, :, :kend]) * scale
        strict = jnp.arange(kend)[None, :] < (q0 + jnp.arange(SB_BLOCK))[:, None]
        log_1mb = jnp.where(strict, jax.nn.log_sigmoid(-z), 0.0)
        between = lax.cumsum(log_1mb, axis=3, reverse=True) - log_1mb
        att = jnp.where(strict, jnp.exp(jax.nn.log_sigmoid(z) + between), 0.0)
        outs.append(jnp.einsum("bhts,bhsd->bhtd", att, v[:, :, :kend]))
    o = jnp.concatenate(outs, axis=2).transpose(0, 2, 1, 3).reshape(B, S, D_MODEL).astype(h.dtype)
    return jnp.einsum("bse,ed->bsd", o, w_out)


def peer_ffn(h, w_query, sub_keys1, sub_keys2, expert_u, expert_v):
    B, S, D = h.shape
    K = PEER_TOPK
    q = jnp.einsum("bsd,de->bse", h, w_query).reshape(B, S, PEER_HEADS, 2, PEER_HALF)
    s1 = jnp.einsum("bshd,nd->bshn", q[..., 0, :], sub_keys1)
    s2 = jnp.einsum("bshd,nd->bshn", q[..., 1, :], sub_keys2)
    v1, i1 = lax.top_k(s1, K)
    v2, i2 = lax.top_k(s2, K)
    cand = (v1[..., :, None] + v2[..., None, :]).reshape(B, S, PEER_HEADS, K * K)
    top_v, top_i = lax.top_k(cand, K)
    e1 = jnp.take_along_axis(i1, top_i // K, axis=-1)
    e2 = jnp.take_along_axis(i2, top_i % K, axis=-1)
    expert = e1 * PEER_NKEYS + e2
    gate = jax.nn.softmax(top_v.astype(jnp.float32), axis=-1).astype(h.dtype)
    nb = (B * S) // PEER_TOKEN_BLOCK
    xb = h.reshape(nb, PEER_TOKEN_BLOCK, D)
    eb = expert.reshape(nb, PEER_TOKEN_BLOCK, PEER_HEADS, K)
    gb = gate.reshape(nb, PEER_TOKEN_BLOCK, PEER_HEADS, K)

    def block(args):
        xt, et, gt = args
        act = jax.nn.gelu(jnp.einsum("td,thkd->thk", xt, expert_u[et]), approximate=False) * gt
        return jnp.einsum("thk,thkd->td", act, expert_v[et])

    return lax.map(block, (xb, eb, gb)).reshape(B, S, D)


def setup_inputs(seed: int = 0) -> dict:
    key = jax.random.key(seed)
    ks = iter(jax.random.split(key, 40))
    nrm = lambda shape, s: jax.random.normal(next(ks), shape, jnp.float32) * s
    gain = lambda n: 1.0 + 0.02 * jax.random.normal(next(ks), (n,), jnp.float32)
    D = D_MODEL
    inp = {}
    inp["x"] = nrm((BATCH, SEQ, D), 1.0)
    inp["c"] = nrm((BATCH, D), 1.0)
    inp["l0_ada_w"] = nrm((D, 6 * D), 0.2 * D ** -0.5)
    inp["l0_ada_b"] = nrm((6 * D,), 0.01)
    inp["l0_norm_mix"] = gain(D)
    inp["l0_w_in"] = nrm((D, IN0_WIDTH), D ** -0.5)
    inp["l0_b_igate"] = nrm((MLSTM_HEADS,), 0.5)
    inp["l0_b_fgate"] = jnp.linspace(3.0, 6.0, MLSTM_HEADS, dtype=jnp.float32) + nrm((MLSTM_HEADS,), 0.1)
    inp["l0_conv_w"] = nrm((CONV_K, CONV_CH), CONV_K ** -0.5)
    inp["l0_mlstm_norm"] = gain(MLSTM_WIDTH)
    inp["l0_w_out"] = nrm((MIX0_WIDTH, D), MIX0_WIDTH ** -0.5)
    inp["l0_norm_ffn"] = gain(D)
    inp["l0_peer_wq"] = nrm((D, PEER_HEADS * PEER_QDIM), D ** -0.5)
    inp["l0_peer_k1"] = nrm((PEER_NKEYS, PEER_HALF), PEER_HALF ** -0.5)
    inp["l0_peer_k2"] = nrm((PEER_NKEYS, PEER_HALF), PEER_HALF ** -0.5)
    inp["l0_peer_u"] = nrm((PEER_EXPERTS, D), D ** -0.5)
    inp["l0_peer_v"] = nrm((PEER_EXPERTS, D), PEER_HEADS ** -0.5)
    inp["l1_ada_w"] = nrm((D, 6 * D), 0.2 * D ** -0.5)
    inp["l1_ada_b"] = nrm((6 * D,), 0.01)
    inp["l1_norm_mix"] = gain(D)
    inp["l1_w_qkv"] = nrm((D, 3 * D), D ** -0.5)
    inp["l1_q_norm"] = gain(SB_HEAD_DIM)
    inp["l1_k_norm"] = gain(SB_HEAD_DIM)
    inp["l1_w_out"] = nrm((D, D), D ** -0.5)
    inp["l1_norm_ffn"] = gain(D)
    inp["l1_peer_wq"] = nrm((D, PEER_HEADS * PEER_QDIM), D ** -0.5)
    inp["l1_peer_k1"] = nrm((PEER_NKEYS, PEER_HALF), PEER_HALF ** -0.5)
    inp["l1_peer_k2"] = nrm((PEER_NKEYS, PEER_HALF), PEER_HALF ** -0.5)
    inp["l1_peer_u"] = nrm((PEER_EXPERTS, D), D ** -0.5)
    inp["l1_peer_v"] = nrm((PEER_EXPERTS, D), PEER_HEADS ** -0.5)
    return inp


def reference(x, c,
              l0_ada_w, l0_ada_b, l0_norm_mix, l0_w_in, l0_b_igate, l0_b_fgate, l0_conv_w,
              l0_mlstm_norm, l0_w_out, l0_norm_ffn, l0_peer_wq, l0_peer_k1, l0_peer_k2,
              l0_peer_u, l0_peer_v,
              l1_ada_w, l1_ada_b, l1_norm_mix, l1_w_qkv, l1_q_norm, l1_k_norm, l1_w_out,
              l1_norm_ffn, l1_peer_wq, l1_peer_k1, l1_peer_k2, l1_peer_u, l1_peer_v):
    layers = [
        dict(ada_w=l0_ada_w, ada_b=l0_ada_b, norm_mix=l0_norm_mix, norm_ffn=l0_norm_ffn,
             peer=(l0_peer_wq, l0_peer_k1, l0_peer_k2, l0_peer_u, l0_peer_v),
             mix=(l0_w_in, l0_b_igate, l0_b_fgate, l0_conv_w, l0_mlstm_norm, l0_w_out)),
        dict(ada_w=l1_ada_w, ada_b=l1_ada_b, norm_mix=l1_norm_mix, norm_ffn=l1_norm_ffn,
             peer=(l1_peer_wq, l1_peer_k1, l1_peer_k2, l1_peer_u, l1_peer_v),
             mix=(l1_w_qkv, l1_q_norm, l1_k_norm, l1_w_out)),
    ]
    for layer in range(DEPTH):
        p = layers[layer]
        sh1, sc1, g1, sh2, sc2, g2 = ada_params(c, p["ada_w"], p["ada_b"])
        h = modulate(x, p["norm_mix"], sh1, sc1)
        if layer % 2 == 0:
            y = mlstm_conv_mixer(h, *p["mix"])
        else:
            y = stick_breaking_mixer(h, *p["mix"])
        x = x + (1 + g1)[:, None, :] * y
        h = modulate(x, p["norm_ffn"], sh2, sc2)
        x = x + (1 + g2)[:, None, :] * peer_ffn(h, *p["peer"])
    return x
```

```python
import functools
import math

import jax
import jax.numpy as jnp
from jax import lax
from jax.experimental import pallas as pl
from jax.experimental.pallas import tpu as pltpu

F32 = jnp.float32
BF16 = jnp.bfloat16
EPS = 1e-6
NT_DIMS = (((1,), (1,)), ((), ()))

LANES = 128
SUBLANES = 8
VMEM_LIMIT_BYTES = 56 * 1024 * 1024

MLSTM_HEADS = 4
MLSTM_CHUNK = 128
SB_HEADS = 16
PEER_HEADS = 8
PEER_TOPK = 16


def _params(*semantics):
    return pltpu.CompilerParams(dimension_semantics=semantics, vmem_limit_bytes=VMEM_LIMIT_BYTES)


def _modulate(x, g, shift, scale):
    ms = jnp.mean(x * x, axis=-1, keepdims=True)
    return x * lax.rsqrt(ms + EPS) * g * (1.0 + scale) + shift


def _log_sigmoid(x):
    return jnp.minimum(x, 0.0) - jnp.log1p(jnp.exp(-jnp.abs(x)))


def _ada_kernel(c_ref, w_ref, b_ref, o_ref):
    c = c_ref[...]
    o_ref[...] = jnp.dot(c * jax.nn.sigmoid(c), w_ref[...], preferred_element_type=F32,
                         precision=lax.Precision.HIGHEST) + b_ref[...]


def _ada(c, w, b):
    bsz, d = c.shape
    n = w.shape[1]
    tn = n // 4
    mod = pl.pallas_call(
        _ada_kernel,
        grid=(n // tn,),
        in_specs=[pl.BlockSpec((bsz, d), lambda j: (0, 0)),
                  pl.BlockSpec((d, tn), lambda j: (0, j)),
                  pl.BlockSpec((1, tn), lambda j: (0, j))],
        out_specs=pl.BlockSpec((bsz, tn), lambda j: (0, j)),
        out_shape=jax.ShapeDtypeStruct((bsz, n), F32),
        compiler_params=_params("arbitrary"),
        name="ada",
    )(c, w, b.reshape(1, n))
    return mod.reshape(bsz, 6, d)


def _l0_in_kernel(x_ref, mod_ref, g_ref, wa_ref, wc_ref, wg_ref, wgt_ref, bgc_ref, bgr_ref, cw_ref,
                  qkvo_ref, hc_ref, gc_ref, gr_ref, carry_ref, *, tm, width):
    @pl.when(pl.program_id(1) == 0)
    def _():
        carry_ref[...] = jnp.zeros_like(carry_ref)

    mod = mod_ref[0]
    h = _modulate(x_ref[0], g_ref[...], mod[0:1], mod[1:2]).astype(BF16)
    qkvo_ref[0] = jnp.dot(h, wa_ref[...], preferred_element_type=F32).astype(BF16)
    gc_ref[0] = jnp.dot(h, wg_ref[...], preferred_element_type=F32) + bgc_ref[...]
    gr_ref[0] = lax.dot_general(wgt_ref[...], h, NT_DIMS, preferred_element_type=F32) + bgr_ref[...]
    pc = jnp.dot(h, wc_ref[...], preferred_element_type=F32)
    cb, cc, ch = pc[:, :width], pc[:, width:2 * width], pc[:, 2 * width:]
    u = cc * ch
    prev = carry_ref[...]
    row = lax.broadcasted_iota(jnp.int32, u.shape, 0)
    u1 = jnp.where(row == 0, prev[SUBLANES - 1:SUBLANES], pltpu.roll(u, 1, 0))
    u2 = jnp.where(row == 0, prev[SUBLANES - 2:SUBLANES - 1],
                   jnp.where(row == 1, prev[SUBLANES - 1:SUBLANES], pltpu.roll(u, 2, 0)))
    cw = cw_ref[...]
    conv = cw[0:1] * u2 + cw[1:2] * u1 + cw[2:3] * u
    hc_ref[0] = (cb * conv).astype(BF16)
    carry_ref[...] = u[tm - SUBLANES:tm]


def _l0_in(x, mod, norm_g, w_in, b_igate, b_fgate, conv_w, tm=512):
    bsz, s, d = x.shape
    hn = MLSTM_HEADS
    width = conv_w.shape[1]
    wa = w_in[:, :4 * width].astype(BF16)
    wg_f = w_in[:, 4 * width:4 * width + 2 * hn]
    wc = w_in[:, 4 * width + 2 * hn:].astype(BF16)
    wg = jnp.pad(wg_f, ((0, 0), (0, LANES - 2 * hn))).astype(BF16)
    wgt = wg_f.T.astype(BF16)
    bias = jnp.concatenate([b_igate, b_fgate])
    bgc = jnp.pad(bias, (0, LANES - 2 * hn)).reshape(1, LANES)
    bgr = bias.reshape(2 * hn, 1)
    full = lambda *shape: pl.BlockSpec(shape, lambda b, i: (0,) * len(shape))
    return pl.pallas_call(
        functools.partial(_l0_in_kernel, tm=tm, width=width),
        grid=(bsz, s // tm),
        in_specs=[pl.BlockSpec((1, tm, d), lambda b, i: (b, i, 0)),
                  pl.BlockSpec((1, 6, d), lambda b, i: (b, 0, 0)),
                  full(1, d), full(d, 4 * width), full(d, 3 * width), full(d, LANES), full(2 * hn, d),
                  full(1, LANES), full(2 * hn, 1), full(3, width)],
        out_specs=[pl.BlockSpec((1, tm, 4 * width), lambda b, i: (b, i, 0)),
                   pl.BlockSpec((1, tm, width), lambda b, i: (b, i, 0)),
                   pl.BlockSpec((1, tm, LANES), lambda b, i: (b, i, 0)),
                   pl.BlockSpec((1, 2 * hn, tm), lambda b, i: (b, 0, i))],
        out_shape=[jax.ShapeDtypeStruct((bsz, s, 4 * width), BF16),
                   jax.ShapeDtypeStruct((bsz, s, width), BF16),
                   jax.ShapeDtypeStruct((bsz, s, LANES), F32),
                   jax.ShapeDtypeStruct((bsz, 2 * hn, s), F32)],
        scratch_shapes=[pltpu.VMEM((SUBLANES, width), F32)],
        compiler_params=_params("arbitrary", "arbitrary"),
        name="l0_in",
    )(x, mod, norm_g.reshape(1, d), wa, wc, wg, wgt, bgc, bgr, conv_w)


def _mlstm_kernel(qkvo_ref, gc_ref, gr_ref, nrm_ref, hm_ref, s_ref, n_ref, m_ref, *, hn, dh, chunk, nchunk):
    @pl.when(pl.program_id(1) == 0)
    def _():
        s_ref[...] = jnp.zeros_like(s_ref)
        n_ref[...] = jnp.zeros_like(n_ref)
        m_ref[...] = jnp.zeros_like(m_ref)

    width = hn * dh
    scale = dh ** -0.5
    ri = lax.broadcasted_iota(jnp.int32, (chunk, chunk), 0)
    ci = lax.broadcasted_iota(jnp.int32, (chunk, chunk), 1)
    causal = ci <= ri
    for c in range(nchunk):
        r0 = c * chunk
        gcol = gc_ref[0, r0:r0 + chunk, :]
        grow = gr_ref[0, :, r0:r0 + chunk]
        for h in range(hn):
            q = qkvo_ref[0, r0:r0 + chunk, h * dh:(h + 1) * dh]
            k = qkvo_ref[0, r0:r0 + chunk, width + h * dh:width + (h + 1) * dh]
            v = qkvo_ref[0, r0:r0 + chunk, 2 * width + h * dh:2 * width + (h + 1) * dh]
            o = qkvo_ref[0, r0:r0 + chunk, 3 * width + h * dh:3 * width + (h + 1) * dh]
            li_col = gcol[:, h:h + 1]
            lf_col = _log_sigmoid(gcol[:, hn + h:hn + h + 1])
            li_row = grow[h:h + 1, :]
            lf_row = _log_sigmoid(grow[hn + h:hn + h + 1, :])
            b_col = jnp.sum(jnp.where(causal, lf_row, 0.0), axis=1, keepdims=True)
            b_row = jnp.sum(jnp.where(ri <= ci, lf_col, 0.0), axis=0, keepdims=True)
            g = b_col[chunk - 1:chunk, :]
            a_col = g - b_col + li_col
            s_prev = s_ref[h]
            n_prev = n_ref[h]
            m_prev = m_ref[h]
            log_d = jnp.where(causal, b_col - b_row + li_row, -jnp.inf)
            inter_log = b_col + m_prev
            m_t = jnp.maximum(inter_log, jnp.max(log_d, axis=1, keepdims=True))
            qk = lax.dot_general(q, k, NT_DIMS, preferred_element_type=F32) * scale * jnp.exp(log_d - m_t)
            inter_w = jnp.exp(inter_log - m_t) * scale
            num = (jnp.dot(qk.astype(BF16), v, preferred_element_type=F32)
                   + inter_w * jnp.dot(q, s_prev.astype(BF16), preferred_element_type=F32))
            den = (jnp.sum(qk, axis=1, keepdims=True)
                   + inter_w * jnp.sum(q.astype(F32) * n_prev, axis=1, keepdims=True))
            hv = num / jnp.maximum(jnp.abs(den), jnp.exp(-m_t))
            hv = hv * lax.rsqrt(jnp.mean(hv * hv, axis=-1, keepdims=True) + EPS) * nrm_ref[:, h * dh:(h + 1) * dh]
            hm_ref[0, r0:r0 + chunk, h * dh:(h + 1) * dh] = (hv * jax.nn.sigmoid(o.astype(F32))).astype(BF16)
            m_new = jnp.maximum(g + m_prev, jnp.max(a_col, axis=0, keepdims=True))
            decay = jnp.exp(g + m_prev - m_new)
            kw = k.astype(F32) * jnp.exp(a_col - m_new)
            s_ref[h] = decay * s_prev + jnp.dot(kw.T.astype(BF16), v, preferred_element_type=F32)
            n_ref[h] = decay * n_prev + jnp.sum(kw, axis=0, keepdims=True)
            m_ref[h] = m_new


def _mlstm(qkvo, gates_col, gates_row, mlstm_norm, tq=512):
    bsz, s, w4 = qkvo.shape
    hn = MLSTM_HEADS
    width = w4 // 4
    dh = width // hn
    return pl.pallas_call(
        functools.partial(_mlstm_kernel, hn=hn, dh=dh, chunk=MLSTM_CHUNK, nchunk=tq // MLSTM_CHUNK),
        grid=(bsz, s // tq),
        in_specs=[pl.BlockSpec((1, tq, w4), lambda b, i: (b, i, 0)),
                  pl.BlockSpec((1, tq, LANES), lambda b, i: (b, i, 0)),
                  pl.BlockSpec((1, 2 * hn, tq), lambda b, i: (b, 0, i)),
                  pl.BlockSpec((1, width), lambda b, i: (0, 0))],
        out_specs=pl.BlockSpec((1, tq, width), lambda b, i: (b, i, 0)),
        out_shape=jax.ShapeDtypeStruct((bsz, s, width), BF16),
        scratch_shapes=[pltpu.VMEM((hn, dh, dh), F32), pltpu.VMEM((hn, 1, dh), F32), pltpu.VMEM((hn, 1, 1), F32)],
        compiler_params=_params("arbitrary", "arbitrary"),
        name="mlstm",
    )(qkvo, gates_col, gates_row, mlstm_norm.reshape(1, width))


def _projres_kernel(*refs, n_act, gate_row):
    acts, ws = refs[:n_act], refs[n_act:2 * n_act]
    x_ref, mod_ref, o_ref = refs[2 * n_act:]
    y = jnp.dot(acts[0][0], ws[0][...], preferred_element_type=F32)
    for a_ref, w_ref in zip(acts[1:], ws[1:]):
        y = y + jnp.dot(a_ref[0], w_ref[...], preferred_element_type=F32)
    gate = mod_ref[0][gate_row:gate_row + 1]
    o_ref[0] = x_ref[0] + (1.0 + gate) * y


def _projres(acts, ws, x, mod, gate_row, tm=512):
    bsz, s, d = x.shape
    n_act = len(acts)
    in_specs = ([pl.BlockSpec((1, tm, a.shape[2]), lambda b, i: (b, i, 0)) for a in acts]
                + [pl.BlockSpec(w.shape, lambda b, i: (0, 0)) for w in ws]
                + [pl.BlockSpec((1, tm, d), lambda b, i: (b, i, 0)),
                   pl.BlockSpec((1, 6, d), lambda b, i: (b, 0, 0))])
    return pl.pallas_call(
        functools.partial(_projres_kernel, n_act=n_act, gate_row=gate_row),
        grid=(bsz, s // tm),
        in_specs=in_specs,
        out_specs=pl.BlockSpec((1, tm, d), lambda b, i: (b, i, 0)),
        out_shape=jax.ShapeDtypeStruct((bsz, s, d), F32),
        compiler_params=_params("arbitrary", "arbitrary"),
        name="projres",
    )(*acts, *ws, x, mod)


def _l1_qkv_kernel(x_ref, mod_ref, g_ref, w_ref, qn_ref, kn_ref, q_ref, k_ref, v_ref, *, d, dh):
    mod = mod_ref[0]
    h = _modulate(x_ref[0], g_ref[...], mod[0:1], mod[1:2]).astype(BF16)
    y = jnp.dot(h, w_ref[...], preferred_element_type=F32)
    tm = y.shape[0]
    low = lax.broadcasted_iota(jnp.int32, (tm, LANES), 1) < dh
    scale = dh ** -0.5

    def headnorm(t, gain):
        sq = t * t
        s_low = jnp.sum(jnp.where(low, sq, 0.0), axis=-1, keepdims=True)
        s_all = jnp.sum(sq, axis=-1, keepdims=True)
        ms = jnp.where(low, s_low, s_all - s_low) * (1.0 / dh)
        return t * lax.rsqrt(ms + EPS) * gain

    for p in range(d // LANES):
        sl = slice(p * LANES, (p + 1) * LANES)
        q_ref[0, :, sl] = (headnorm(y[:, sl], qn_ref[...]) * scale).astype(BF16)
        k_ref[0, :, sl] = headnorm(y[:, d + p * LANES:d + (p + 1) * LANES], kn_ref[...]).astype(BF16)
    v_ref[0] = y[:, 2 * d:].astype(BF16)


def _l1_qkv(x, mod, norm_g, w_qkv, q_norm, k_norm, tm=512):
    bsz, s, d = x.shape
    dh = q_norm.shape[0]
    assert LANES == 2 * dh
    full = lambda *shape: pl.BlockSpec(shape, lambda b, i: (0,) * len(shape))
    tok = pl.BlockSpec((1, tm, d), lambda b, i: (b, i, 0))
    return pl.pallas_call(
        functools.partial(_l1_qkv_kernel, d=d, dh=dh),
        grid=(bsz, s // tm),
        in_specs=[tok, pl.BlockSpec((1, 6, d), lambda b, i: (b, 0, 0)), full(1, d), full(d, 3 * d),
                  full(1, LANES), full(1, LANES)],
        out_specs=[tok, tok, tok],
        out_shape=[jax.ShapeDtypeStruct((bsz, s, d), BF16)] * 3,
        compiler_params=_params("arbitrary", "arbitrary"),
        name="l1_qkv",
    )(x, mod, norm_g.reshape(1, d), w_qkv.astype(BF16), jnp.tile(q_norm, 2).reshape(1, LANES),
      jnp.tile(k_norm, 2).reshape(1, LANES))


def _sb_kernel(q_ref, k_ref, v_ref, o_ref, *, tq, dh):
    qi = pl.program_id(2)
    q = q_ref[0]
    lane = lax.broadcasted_iota(jnp.int32, (tq, LANES), 1)
    ri = lax.broadcasted_iota(jnp.int32, (tq, tq), 0)
    ci = lax.broadcasted_iota(jnp.int32, (tq, tq), 1)
    strict = ci < ri
    rhs = jnp.concatenate([jnp.where(ri > ci, 1.0, 0.0), jnp.ones((tq, LANES), F32)], axis=1).astype(BF16)
    rep = tq // LANES

    def block(kb, carry, acc, qm, diag):
        start = pl.multiple_of(kb * tq, tq)
        kblk = k_ref[0, pl.ds(start, tq), :]
        vblk = v_ref[0, pl.ds(start, tq), :]
        z = lax.dot_general(qm, kblk, NT_DIMS, preferred_element_type=F32)
        lsn = _log_sigmoid(-z)
        if diag:
            lsn = jnp.where(strict, lsn, 0.0)
        hi = lsn.astype(BF16)
        lo = (lsn - hi.astype(F32)).astype(BF16)
        rr = jnp.dot(hi, rhs, preferred_element_type=F32) + jnp.dot(lo, rhs, preferred_element_type=F32)
        logit = z + lsn + rr[:, :tq] + jnp.concatenate([carry] * rep, axis=1)
        p = jnp.exp(logit)
        if diag:
            p = jnp.where(strict, p, 0.0)
        acc = acc + jnp.dot(p.astype(BF16), vblk, preferred_element_type=F32)
        return carry + rr[:, tq:], acc

    out = None
    for hh in range(LANES // dh):
        hmask = (lane >= hh * dh) & (lane < (hh + 1) * dh)
        qm = jnp.where(hmask, q, jnp.zeros_like(q))
        carry, acc = block(qi, jnp.zeros((tq, LANES), F32), jnp.zeros((tq, LANES), F32), qm, True)

        def body(it, ca, qm=qm):
            return block(qi - 1 - it, ca[0], ca[1], qm, False)

        carry, acc = lax.fori_loop(0, qi, body, (carry, acc))
        out = acc if out is None else jnp.where(hmask, acc, out)
    o_ref[0] = out.astype(BF16)


def _sb_attention(q, k, v, tq=256):
    bsz, s, d = q.shape
    dh = d // SB_HEADS
    npair = d // LANES
    return pl.pallas_call(
        functools.partial(_sb_kernel, tq=tq, dh=dh),
        grid=(bsz, npair, s // tq),
        in_specs=[pl.BlockSpec((1, tq, LANES), lambda b, p, i: (b, i, p)),
                  pl.BlockSpec((1, s, LANES), lambda b, p, i: (b, 0, p)),
                  pl.BlockSpec((1, s, LANES), lambda b, p, i: (b, 0, p))],
        out_specs=pl.BlockSpec((1, tq, LANES), lambda b, p, i: (b, i, p)),
        out_shape=jax.ShapeDtypeStruct((bsz, s, d), BF16),
        compiler_params=_params("arbitrary", "arbitrary", "arbitrary"),
        name="sb_attn",
    )(q, k, v)


def _extract_top(cur, rowf, n_rows, on_pick, state):
    def body(i, carry):
        cur, state = carry
        mx = jnp.max(cur, axis=0, keepdims=True)
        idx = jnp.min(jnp.where(cur == mx, rowf, float(n_rows)), axis=0, keepdims=True)
        sel = rowf == idx
        return jnp.where(sel, -jnp.inf, cur), on_pick(i, mx, sel, state)

    return lax.fori_loop(0, PEER_TOPK, body, (cur, state))[1]


def _peer_query_kernel(x_ref, mod_ref, g_ref, wq_ref, k1_ref, k2_ref,
                       h_ref, a1_ref, c1_ref, b1_ref, r2_ref, v1_ref, v2_ref, cand_ref, *, heads, nk):
    topk = PEER_TOPK
    mod = mod_ref[0]
    h = _modulate(x_ref[0], g_ref[...], mod[3:4], mod[4:5]).astype(BF16)
    h_ref[0] = h
    qall = jnp.dot(h, wq_ref[...], preferred_element_type=F32).astype(BF16)
    tp = qall.shape[0]
    rowk = lax.broadcasted_iota(jnp.int32, (nk, tp), 0).astype(F32)
    pairs = [(i, topk // (i + 1)) for i in range(topk)]
    ncand = sum(n for _, n in pairs)
    ncand_pad = cand_ref.shape[0]
    rowc = lax.broadcasted_iota(jnp.int32, (ncand_pad, tp), 0).astype(F32)

    def rank_and_values(s, vals_ref):
        def on_pick(i, mx, sel, rank):
            vals_ref[pl.ds(i, 1), :] = mx
            return jnp.where(sel, i.astype(F32), rank)

        return _extract_top(s, rowk, nk, on_pick, jnp.full(s.shape, float(topk), F32))

    for hd in range(heads):
        q1 = qall[:, hd * 2 * nk:hd * 2 * nk + nk]
        q2 = qall[:, hd * 2 * nk + nk:(hd + 1) * 2 * nk]
        s1 = lax.dot_general(k1_ref[...], q1, NT_DIMS, preferred_element_type=F32)
        s2 = lax.dot_general(k2_ref[...], q2, NT_DIMS, preferred_element_type=F32)
        r1 = rank_and_values(s1, v1_ref)
        r2 = rank_and_values(s2, v2_ref)
        off = 0
        for i, n in pairs:
            cand_ref[off:off + n, :] = v1_ref[i:i + 1, :] + v2_ref[0:n, :]
            off += n
        cand_ref[ncand:ncand_pad, :] = jnp.full((ncand_pad - ncand, tp), -jnp.inf, F32)
        cand = cand_ref[...]
        picked = _extract_top(cand, rowc, ncand_pad, lambda i, mx, sel, acc: jnp.where(sel, 1.0, acc),
                              jnp.zeros(cand.shape, F32))
        z = jnp.sum(picked * jnp.exp(cand - cand[0:1]), axis=0, keepdims=True)
        a1 = jnp.where(r1 < topk, jnp.exp(s1 - v1_ref[0:1, :]), 0.0) / z
        cnt1 = jnp.zeros(s1.shape, F32)
        off = 0
        for i, n in pairs:
            c_i = jnp.sum(picked[off:off + n], axis=0, keepdims=True)
            cnt1 = jnp.where(r1 == float(i), c_i, cnt1)
            off += n
        a1_ref[hd] = a1
        c1_ref[hd] = cnt1
        b1_ref[hd] = jnp.where(r2 < topk, jnp.exp(s2 - v2_ref[0:1, :]), 0.0)
        r2_ref[hd] = r2


def _peer_query(x, mod, norm_g, wq, k1, k2, tp=256):
    bsz, s, d = x.shape
    nk = k1.shape[0]
    heads = wq.shape[1] // (2 * k1.shape[1])
    assert heads == PEER_HEADS and k1.shape[1] == nk
    ntok = bsz * s
    nt = s // tp
    full = lambda *shape: pl.BlockSpec(shape, lambda b, i: (0,) * len(shape))
    gate_spec = pl.BlockSpec((heads, nk, tp), lambda b, i: (0, 0, b * nt + i))
    gate_shape = jax.ShapeDtypeStruct((heads, nk, ntok), F32)
    ncand_pad = 56
    return pl.pallas_call(
        functools.partial(_peer_query_kernel, heads=heads, nk=nk),
        grid=(bsz, nt),
        in_specs=[pl.BlockSpec((1, tp, d), lambda b, i: (b, i, 0)),
                  pl.BlockSpec((1, 6, d), lambda b, i: (b, 0, 0)),
                  full(1, d), full(d, wq.shape[1]), full(nk, nk), full(nk, nk)],
        out_specs=[pl.BlockSpec((1, tp, d), lambda b, i: (b, i, 0)), gate_spec, gate_spec, gate_spec, gate_spec],
        out_shape=[jax.ShapeDtypeStruct((bsz, s, d), BF16), gate_shape, gate_shape, gate_shape, gate_shape],
        scratch_shapes=[pltpu.VMEM((PEER_TOPK, tp), F32), pltpu.VMEM((PEER_TOPK, tp), F32),
                        pltpu.VMEM((ncand_pad, tp), F32)],
        compiler_params=_params("arbitrary", "arbitrary"),
        name="peer_query",
    )(x, mod, norm_g.reshape(1, d), wq.astype(BF16), k1.astype(BF16), k2.astype(BF16))


def _peer_expert_kernel(h_ref, u_ref, vt_ref, a1_ref, c1_ref, b1_ref, r2_ref, x_ref, mod_ref, o_ref,
                        s_ref, w_ref, acc_ref, *, heads, nk, n_e1, tl):
    j = pl.program_id(1)

    @pl.when(j == 0)
    def _():
        acc_ref[...] = jnp.zeros_like(acc_ref)

    s_ref[...] = lax.dot_general(u_ref[...], h_ref[...], NT_DIMS, preferred_element_type=F32)
    tt = h_ref.shape[0]

    def per_token_tile(t, _):
        cols = pl.ds(pl.multiple_of(t * tl, tl), tl)
        for e in range(n_e1):
            rows = slice(e * nk, (e + 1) * nk)
            s = s_ref[rows, cols]
            act = 0.5 * s * (1.0 + lax.erf(s * (1.0 / math.sqrt(2.0))))
            gate = jnp.zeros((nk, tl), F32)
            for hd in range(heads):
                cnt = c1_ref[hd, e:e + 1, cols]
                a1 = a1_ref[hd, e:e + 1, cols]
                gate = gate + jnp.where(r2_ref[hd, :, cols] < cnt, b1_ref[hd, :, cols], 0.0) * a1
            w_ref[rows, cols] = (act * gate).astype(BF16)
        return 0

    lax.fori_loop(0, tt // tl, per_token_tile, 0)
    acc_ref[...] += jnp.dot(vt_ref[...], w_ref[...], preferred_element_type=F32)

    @pl.when(j == pl.num_programs(1) - 1)
    def _():
        o_ref[...] = x_ref[...] + (1.0 + mod_ref[0][5:6]) * acc_ref[...].T


def _peer_experts(h2, a1, c1, b1, r2, x, mod, expert_u, expert_v, tt=512, ec=1024, tl=128):
    bsz, s, d = x.shape
    ntok = bsz * s
    heads, nk, _ = a1.shape
    n_exp = expert_u.shape[0]
    n_e1 = ec // nk
    tiles_per_seq = s // tt
    u = expert_u.astype(BF16)
    vt = expert_v.T.astype(BF16)
    out = pl.pallas_call(
        functools.partial(_peer_expert_kernel, heads=heads, nk=nk, n_e1=n_e1, tl=tl),
        grid=(ntok // tt, n_exp // ec),
        in_specs=[pl.BlockSpec((tt, d), lambda i, j: (i, 0)),
                  pl.BlockSpec((ec, d), lambda i, j: (j, 0)),
                  pl.BlockSpec((d, ec), lambda i, j: (0, j)),
                  pl.BlockSpec((heads, n_e1, tt), lambda i, j: (0, j, i)),
                  pl.BlockSpec((heads, n_e1, tt), lambda i, j: (0, j, i)),
                  pl.BlockSpec((heads, nk, tt), lambda i, j: (0, 0, i)),
                  pl.BlockSpec((heads, nk, tt), lambda i, j: (0, 0, i)),
                  pl.BlockSpec((tt, d), lambda i, j: (i, 0)),
                  pl.BlockSpec((1, 6, d), lambda i, j: (i // tiles_per_seq, 0, 0))],
        out_specs=pl.BlockSpec((tt, d), lambda i, j: (i, 0)),
        out_shape=jax.ShapeDtypeStruct((ntok, d), F32),
        scratch_shapes=[pltpu.VMEM((ec, tt), F32), pltpu.VMEM((ec, tt), BF16), pltpu.VMEM((d, tt), F32)],
        compiler_params=_params("arbitrary", "arbitrary"),
        name="peer_experts",
    )(h2.reshape(ntok, d), u, vt, a1, c1, b1, r2, x.reshape(ntok, d), mod)
    return out.reshape(bsz, s, d)


def _peer_ffn(x, mod, norm_g, wq, k1, k2, expert_u, expert_v):
    h2, a1, c1, b1, r2 = _peer_query(x, mod, norm_g, wq, k1, k2)
    return _peer_experts(h2, a1, c1, b1, r2, x, mod, expert_u, expert_v)


def kernel(x, c, l0_ada_w, l0_ada_b, l0_norm_mix, l0_w_in, l0_b_igate, l0_b_fgate, l0_conv_w, l0_mlstm_norm, l0_w_out, l0_norm_ffn, l0_peer_wq, l0_peer_k1, l0_peer_k2, l0_peer_u, l0_peer_v, l1_ada_w, l1_ada_b, l1_norm_mix, l1_w_qkv, l1_q_norm, l1_k_norm, l1_w_out, l1_norm_ffn, l1_peer_wq, l1_peer_k1, l1_peer_k2, l1_peer_u, l1_peer_v):
    mod0 = _ada(c, l0_ada_w, l0_ada_b)
    qkvo, hc, gates_col, gates_row = _l0_in(x, mod0, l0_norm_mix, l0_w_in, l0_b_igate, l0_b_fgate, l0_conv_w)
    hm = _mlstm(qkvo, gates_col, gates_row, l0_mlstm_norm)
    width = hm.shape[2]
    w_out0 = l0_w_out.astype(BF16)
    x = _projres([hm, hc], [w_out0[:width], w_out0[width:]], x, mod0, gate_row=2)
    x = _peer_ffn(x, mod0, l0_norm_ffn, l0_peer_wq, l0_peer_k1, l0_peer_k2, l0_peer_u, l0_peer_v)
    mod1 = _ada(c, l1_ada_w, l1_ada_b)
    q, k, v = _l1_qkv(x, mod1, l1_norm_mix, l1_w_qkv, l1_q_norm, l1_k_norm)
    o = _sb_attention(q, k, v)
    x = _projres([o], [l1_w_out.astype(BF16)], x, mod1, gate_row=2)
    x = _peer_ffn(x, mod1, l1_norm_ffn, l1_peer_wq, l1_peer_k1, l1_peer_k2, l1_peer_u, l1_peer_v)
    return x
```

```python
import functools
import math

import jax
import jax.numpy as jnp
from jax import lax
from jax.experimental import pallas as pl
from jax.experimental.pallas import tpu as pltpu

F32 = jnp.float32
BF16 = jnp.bfloat16
EPS = 1e-6
NT_DIMS = (((1,), (1,)), ((), ()))

LANES = 128
SUBLANES = 8
VMEM_LIMIT_BYTES = 56 * 1024 * 1024

MLSTM_HEADS = 4
MLSTM_CHUNK = 128
SB_HEADS = 16
PEER_HEADS = 8
PEER_TOPK = 16
ATT_LOG_CUTOFF = 100.0


def _params(*semantics):
    return pltpu.CompilerParams(dimension_semantics=semantics, vmem_limit_bytes=VMEM_LIMIT_BYTES)


def _modulate(x, g, shift, scale):
    ms = jnp.mean(x * x, axis=-1, keepdims=True)
    return x * lax.rsqrt(ms + EPS) * g * (1.0 + scale) + shift


def _log_sigmoid(x):
    return jnp.minimum(x, 0.0) - jnp.log1p(jnp.exp(-jnp.abs(x)))


def _ada_kernel(c_ref, w_ref, b_ref, o_ref):
    c = c_ref[...]
    o_ref[...] = jnp.dot(c * jax.nn.sigmoid(c), w_ref[...], preferred_element_type=F32,
                         precision=lax.Precision.HIGHEST) + b_ref[...]


def _ada(c, w, b):
    bsz, d = c.shape
    n = w.shape[1]
    tn = n // 4
    mod = pl.pallas_call(
        _ada_kernel,
        grid=(n // tn,),
        in_specs=[pl.BlockSpec((bsz, d), lambda j: (0, 0)),
                  pl.BlockSpec((d, tn), lambda j: (0, j)),
                  pl.BlockSpec((1, tn), lambda j: (0, j))],
        out_specs=pl.BlockSpec((bsz, tn), lambda j: (0, j)),
        out_shape=jax.ShapeDtypeStruct((bsz, n), F32),
        compiler_params=_params("arbitrary"),
        name="ada",
    )(c, w, b.reshape(1, n))
    return mod.reshape(bsz, 6, d)


def _l0_in_kernel(x_ref, mod_ref, g_ref, wa_ref, wc_ref, wg_ref, wgt_ref, bgc_ref, bgr_ref, cw_ref,
                  qkvo_ref, hc_ref, gc_ref, gr_ref, carry_ref, *, tm, width):
    @pl.when(pl.program_id(1) == 0)
    def _():
        carry_ref[...] = jnp.zeros_like(carry_ref)

    mod = mod_ref[0]
    h = _modulate(x_ref[0], g_ref[...], mod[0:1], mod[1:2]).astype(BF16)
    qkvo_ref[0] = jnp.dot(h, wa_ref[...], preferred_element_type=F32).astype(BF16)
    gc_ref[0] = jnp.dot(h, wg_ref[...], preferred_element_type=F32) + bgc_ref[...]
    gr_ref[0] = lax.dot_general(wgt_ref[...], h, NT_DIMS, preferred_element_type=F32) + bgr_ref[...]
    pc = jnp.dot(h, wc_ref[...], preferred_element_type=F32)
    cb, cc, ch = pc[:, :width], pc[:, width:2 * width], pc[:, 2 * width:]
    u = cc * ch
    prev = carry_ref[...]
    row = lax.broadcasted_iota(jnp.int32, u.shape, 0)
    u1 = jnp.where(row == 0, prev[SUBLANES - 1:SUBLANES], pltpu.roll(u, 1, 0))
    u2 = jnp.where(row == 0, prev[SUBLANES - 2:SUBLANES - 1],
                   jnp.where(row == 1, prev[SUBLANES - 1:SUBLANES], pltpu.roll(u, 2, 0)))
    cw = cw_ref[...]
    conv = cw[0:1] * u2 + cw[1:2] * u1 + cw[2:3] * u
    hc_ref[0] = (cb * conv).astype(BF16)
    carry_ref[...] = u[tm - SUBLANES:tm]


def _l0_in(x, mod, norm_g, w_in, b_igate, b_fgate, conv_w, tm=512):
    bsz, s, d = x.shape
    hn = MLSTM_HEADS
    width = conv_w.shape[1]
    wa = w_in[:, :4 * width].astype(BF16)
    wg_f = w_in[:, 4 * width:4 * width + 2 * hn]
    wc = w_in[:, 4 * width + 2 * hn:].astype(BF16)
    wg = jnp.pad(wg_f, ((0, 0), (0, LANES - 2 * hn))).astype(BF16)
    wgt = wg_f.T.astype(BF16)
    bias = jnp.concatenate([b_igate, b_fgate])
    bgc = jnp.pad(bias, (0, LANES - 2 * hn)).reshape(1, LANES)
    bgr = bias.reshape(2 * hn, 1)
    full = lambda *shape: pl.BlockSpec(shape, lambda b, i: (0,) * len(shape))
    return pl.pallas_call(
        functools.partial(_l0_in_kernel, tm=tm, width=width),
        grid=(bsz, s // tm),
        in_specs=[pl.BlockSpec((1, tm, d), lambda b, i: (b, i, 0)),
                  pl.BlockSpec((1, 6, d), lambda b, i: (b, 0, 0)),
                  full(1, d), full(d, 4 * width), full(d, 3 * width), full(d, LANES), full(2 * hn, d),
                  full(1, LANES), full(2 * hn, 1), full(3, width)],
        out_specs=[pl.BlockSpec((1, tm, 4 * width), lambda b, i: (b, i, 0)),
                   pl.BlockSpec((1, tm, width), lambda b, i: (b, i, 0)),
                   pl.BlockSpec((1, tm, LANES), lambda b, i: (b, i, 0)),
                   pl.BlockSpec((1, 2 * hn, tm), lambda b, i: (b, 0, i))],
        out_shape=[jax.ShapeDtypeStruct((bsz, s, 4 * width), BF16),
                   jax.ShapeDtypeStruct((bsz, s, width), BF16),
                   jax.ShapeDtypeStruct((bsz, s, LANES), F32),
                   jax.ShapeDtypeStruct((bsz, 2 * hn, s), F32)],
        scratch_shapes=[pltpu.VMEM((SUBLANES, width), F32)],
        compiler_params=_params("arbitrary", "arbitrary"),
        name="l0_in",
    )(x, mod, norm_g.reshape(1, d), wa, wc, wg, wgt, bgc, bgr, conv_w)


def _mlstm_kernel(qkvo_ref, gc_ref, gr_ref, nrm_ref, hm_ref, s_ref, n_ref, m_ref, *, hn, dh, chunk, nchunk):
    @pl.when(pl.program_id(1) == 0)
    def _():
        s_ref[...] = jnp.zeros_like(s_ref)
        n_ref[...] = jnp.zeros_like(n_ref)
        m_ref[...] = jnp.zeros_like(m_ref)

    width = hn * dh
    scale = dh ** -0.5
    ri = lax.broadcasted_iota(jnp.int32, (chunk, chunk), 0)
    ci = lax.broadcasted_iota(jnp.int32, (chunk, chunk), 1)
    causal = ci <= ri
    for c in range(nchunk):
        r0 = c * chunk
        gcol = gc_ref[0, r0:r0 + chunk, :]
        grow = gr_ref[0, :, r0:r0 + chunk]
        for h in range(hn):
            q = qkvo_ref[0, r0:r0 + chunk, h * dh:(h + 1) * dh]
            k = qkvo_ref[0, r0:r0 + chunk, width + h * dh:width + (h + 1) * dh]
            v = qkvo_ref[0, r0:r0 + chunk, 2 * width + h * dh:2 * width + (h + 1) * dh]
            o = qkvo_ref[0, r0:r0 + chunk, 3 * width + h * dh:3 * width + (h + 1) * dh]
            li_col = gcol[:, h:h + 1]
            lf_col = _log_sigmoid(gcol[:, hn + h:hn + h + 1])
            li_row = grow[h:h + 1, :]
            lf_row = _log_sigmoid(grow[hn + h:hn + h + 1, :])
            b_col = jnp.sum(jnp.where(causal, lf_row, 0.0), axis=1, keepdims=True)
            b_row = jnp.sum(jnp.where(ri <= ci, lf_col, 0.0), axis=0, keepdims=True)
            g = b_col[chunk - 1:chunk, :]
            a_col = g - b_col + li_col
            s_prev = s_ref[h]
            n_prev = n_ref[h]
            m_prev = m_ref[h]
            log_d = jnp.where(causal, b_col - b_row + li_row, -jnp.inf)
            inter_log = b_col + m_prev
            m_t = jnp.maximum(inter_log, jnp.max(log_d, axis=1, keepdims=True))
            qk = lax.dot_general(q, k, NT_DIMS, preferred_element_type=F32) * scale * jnp.exp(log_d - m_t)
            inter_w = jnp.exp(inter_log - m_t) * scale
            num = (jnp.dot(qk.astype(BF16), v, preferred_element_type=F32)
                   + inter_w * jnp.dot(q, s_prev.astype(BF16), preferred_element_type=F32))
            den = (jnp.sum(qk, axis=1, keepdims=True)
                   + inter_w * jnp.sum(q.astype(F32) * n_prev, axis=1, keepdims=True))
            hv = num / jnp.maximum(jnp.abs(den), jnp.exp(-m_t))
            hv = hv * lax.rsqrt(jnp.mean(hv * hv, axis=-1, keepdims=True) + EPS) * nrm_ref[:, h * dh:(h + 1) * dh]
            hm_ref[0, r0:r0 + chunk, h * dh:(h + 1) * dh] = (hv * jax.nn.sigmoid(o.astype(F32))).astype(BF16)
            m_new = jnp.maximum(g + m_prev, jnp.max(a_col, axis=0, keepdims=True))
            decay = jnp.exp(g + m_prev - m_new)
            kw = k.astype(F32) * jnp.exp(a_col - m_new)
            s_ref[h] = decay * s_prev + jnp.dot(kw.T.astype(BF16), v, preferred_element_type=F32)
            n_ref[h] = decay * n_prev + jnp.sum(kw, axis=0, keepdims=True)
            m_ref[h] = m_new


def _mlstm(qkvo, gates_col, gates_row, mlstm_norm, tq=512):
    bsz, s, w4 = qkvo.shape
    hn = MLSTM_HEADS
    width = w4 // 4
    dh = width // hn
    return pl.pallas_call(
        functools.partial(_mlstm_kernel, hn=hn, dh=dh, chunk=MLSTM_CHUNK, nchunk=tq // MLSTM_CHUNK),
        grid=(bsz, s // tq),
        in_specs=[pl.BlockSpec((1, tq, w4), lambda b, i: (b, i, 0)),
                  pl.BlockSpec((1, tq, LANES), lambda b, i: (b, i, 0)),
                  pl.BlockSpec((1, 2 * hn, tq), lambda b, i: (b, 0, i)),
                  pl.BlockSpec((1, width), lambda b, i: (0, 0))],
        out_specs=pl.BlockSpec((1, tq, width), lambda b, i: (b, i, 0)),
        out_shape=jax.ShapeDtypeStruct((bsz, s, width), BF16),
        scratch_shapes=[pltpu.VMEM((hn, dh, dh), F32), pltpu.VMEM((hn, 1, dh), F32), pltpu.VMEM((hn, 1, 1), F32)],
        compiler_params=_params("arbitrary", "arbitrary"),
        name="mlstm",
    )(qkvo, gates_col, gates_row, mlstm_norm.reshape(1, width))


def _projres_kernel(*refs, n_act, gate_row):
    acts, ws = refs[:n_act], refs[n_act:2 * n_act]
    x_ref, mod_ref, o_ref = refs[2 * n_act:]
    y = jnp.dot(acts[0][0], ws[0][...], preferred_element_type=F32)
    for a_ref, w_ref in zip(acts[1:], ws[1:]):
        y = y + jnp.dot(a_ref[0], w_ref[...], preferred_element_type=F32)
    gate = mod_ref[0][gate_row:gate_row + 1]
    o_ref[0] = x_ref[0] + (1.0 + gate) * y


def _projres(acts, ws, x, mod, gate_row, tm=512):
    bsz, s, d = x.shape
    n_act = len(acts)
    in_specs = ([pl.BlockSpec((1, tm, a.shape[2]), lambda b, i: (b, i, 0)) for a in acts]
                + [pl.BlockSpec(w.shape, lambda b, i: (0, 0)) for w in ws]
                + [pl.BlockSpec((1, tm, d), lambda b, i: (b, i, 0)),
                   pl.BlockSpec((1, 6, d), lambda b, i: (b, 0, 0))])
    return pl.pallas_call(
        functools.partial(_projres_kernel, n_act=n_act, gate_row=gate_row),
        grid=(bsz, s // tm),
        in_specs=in_specs,
        out_specs=pl.BlockSpec((1, tm, d), lambda b, i: (b, i, 0)),
        out_shape=jax.ShapeDtypeStruct((bsz, s, d), F32),
        compiler_params=_params("arbitrary", "arbitrary"),
        name="projres",
    )(*acts, *ws, x, mod)


def _l1_qkv_kernel(x_ref, mod_ref, g_ref, w_ref, qn_ref, kn_ref, q_ref, k_ref, v_ref, *, d, dh):
    mod = mod_ref[0]
    h = _modulate(x_ref[0], g_ref[...], mod[0:1], mod[1:2]).astype(BF16)
    y = jnp.dot(h, w_ref[...], preferred_element_type=F32)
    tm = y.shape[0]
    low = lax.broadcasted_iota(jnp.int32, (tm, LANES), 1) < dh
    scale = dh ** -0.5

    def headnorm(t, gain):
        sq = t * t
        s_low = jnp.sum(jnp.where(low, sq, 0.0), axis=-1, keepdims=True)
        s_all = jnp.sum(sq, axis=-1, keepdims=True)
        ms = jnp.where(low, s_low, s_all - s_low) * (1.0 / dh)
        return t * lax.rsqrt(ms + EPS) * gain

    for p in range(d // LANES):
        sl = slice(p * LANES, (p + 1) * LANES)
        q_ref[0, :, sl] = (headnorm(y[:, sl], qn_ref[...]) * scale).astype(BF16)
        k_ref[0, :, sl] = headnorm(y[:, d + p * LANES:d + (p + 1) * LANES], kn_ref[...]).astype(BF16)
    v_ref[0] = y[:, 2 * d:].astype(BF16)


def _l1_qkv(x, mod, norm_g, w_qkv, q_norm, k_norm, tm=512):
    bsz, s, d = x.shape
    dh = q_norm.shape[0]
    assert LANES == 2 * dh
    full = lambda *shape: pl.BlockSpec(shape, lambda b, i: (0,) * len(shape))
    tok = pl.BlockSpec((1, tm, d), lambda b, i: (b, i, 0))
    return pl.pallas_call(
        functools.partial(_l1_qkv_kernel, d=d, dh=dh),
        grid=(bsz, s // tm),
        in_specs=[tok, pl.BlockSpec((1, 6, d), lambda b, i: (b, 0, 0)), full(1, d), full(d, 3 * d),
                  full(1, LANES), full(1, LANES)],
        out_specs=[tok, tok, tok],
        out_shape=[jax.ShapeDtypeStruct((bsz, s, d), BF16)] * 3,
        compiler_params=_params("arbitrary", "arbitrary"),
        name="l1_qkv",
    )(x, mod, norm_g.reshape(1, d), w_qkv.astype(BF16), jnp.tile(q_norm, 2).reshape(1, LANES),
      jnp.tile(k_norm, 2).reshape(1, LANES))


def _sb_kernel(q_ref, k_ref, v_ref, o_ref, carry_ref, acc_ref, *, tq, dh):
    qi = pl.program_id(2)
    q = q_ref[0]
    lane = lax.broadcasted_iota(jnp.int32, (tq, LANES), 1)
    ri = lax.broadcasted_iota(jnp.int32, (tq, tq), 0)
    ci = lax.broadcasted_iota(jnp.int32, (tq, tq), 1)
    strict = ci < ri
    rhs = jnp.concatenate([jnp.where(ri > ci, 1.0, 0.0), jnp.ones((tq, LANES), F32)], axis=1).astype(BF16)
    rep = tq // LANES
    n_heads = LANES // dh
    hmasks = [(lane >= hh * dh) & (lane < (hh + 1) * dh) for hh in range(n_heads)]
    qms = [jnp.where(m, q, jnp.zeros_like(q)) for m in hmasks]

    def block(kb, diag):
        start = pl.multiple_of(kb * tq, tq)
        kblk = k_ref[0, pl.ds(start, tq), :]
        vblk = v_ref[0, pl.ds(start, tq), :]
        worst = None
        for hh in range(n_heads):
            z = lax.dot_general(qms[hh], kblk, NT_DIMS, preferred_element_type=F32)
            lsn = _log_sigmoid(-z)
            if diag:
                lsn = jnp.where(strict, lsn, 0.0)
            hi = lsn.astype(BF16)
            lo = (lsn - hi.astype(F32)).astype(BF16)
            rr = jnp.dot(hi, rhs, preferred_element_type=F32) + jnp.dot(lo, rhs, preferred_element_type=F32)
            logit = z + lsn + rr[:, :tq]
            if not diag:
                logit = logit + jnp.concatenate([carry_ref[hh]] * rep, axis=1)
            p = jnp.exp(logit)
            if diag:
                p = jnp.where(strict, p, 0.0)
            pv = jnp.dot(p.astype(BF16), vblk, preferred_element_type=F32)
            if diag:
                acc_ref[hh] = pv
                carry = rr[:, tq:]
            else:
                acc_ref[hh] += pv
                carry = carry_ref[hh] + rr[:, tq:]
            carry_ref[hh] = carry
            top = jnp.max(carry)
            worst = top if worst is None else jnp.maximum(worst, top)
        return worst

    def cond(state):
        it, worst = state
        return (it < qi) & (worst > -ATT_LOG_CUTOFF)

    def body(state):
        it, _ = state
        return it + 1, block(qi - 1 - it, False)

    lax.while_loop(cond, body, (jnp.int32(0), block(qi, True)))
    out = acc_ref[0]
    for hh in range(1, n_heads):
        out = jnp.where(hmasks[hh], acc_ref[hh], out)
    o_ref[0] = out.astype(BF16)


def _sb_attention(q, k, v, tq=256):
    bsz, s, d = q.shape
    dh = d // SB_HEADS
    npair = d // LANES
    return pl.pallas_call(
        functools.partial(_sb_kernel, tq=tq, dh=dh),
        grid=(bsz, npair, s // tq),
        in_specs=[pl.BlockSpec((1, tq, LANES), lambda b, p, i: (b, i, p)),
                  pl.BlockSpec((1, s, LANES), lambda b, p, i: (b, 0, p)),
                  pl.BlockSpec((1, s, LANES), lambda b, p, i: (b, 0, p))],
        out_specs=pl.BlockSpec((1, tq, LANES), lambda b, p, i: (b, i, p)),
        out_shape=jax.ShapeDtypeStruct((bsz, s, d), BF16),
        scratch_shapes=[pltpu.VMEM((LANES // dh, tq, LANES), F32), pltpu.VMEM((LANES // dh, tq, LANES), F32)],
        compiler_params=_params("arbitrary", "arbitrary", "arbitrary"),
        name="sb_attn",
    )(q, k, v)


def _rank_top(s_ref, rank_ref, vals_ref):
    n, width = s_ref.shape
    rowf = lax.broadcasted_iota(jnp.int32, (n, LANES), 0).astype(F32)

    slot = lax.broadcasted_iota(jnp.int32, (PEER_TOPK, LANES), 0)

    def run(cols, exact):
        def body(i, carry):
            cur, rank, vals = carry
            mx = jnp.max(cur, axis=0, keepdims=True)
            hit = cur == mx
            if exact:
                hit = rowf == jnp.min(jnp.where(hit, rowf, float(n)), axis=0, keepdims=True)
            return (jnp.where(hit, -jnp.inf, cur), jnp.where(hit, i.astype(F32), rank),
                    jnp.where(slot == i, mx, vals))

        init = (s_ref[:, cols], jnp.full((n, LANES), float(PEER_TOPK), F32), jnp.zeros((PEER_TOPK, LANES), F32))
        _, rank, vals = lax.fori_loop(0, PEER_TOPK, body, init)
        rank_ref[:, cols] = rank
        vals_ref[:, cols] = vals

    for c0 in range(0, width, LANES):
        cols = slice(c0, c0 + LANES)
        run(cols, exact=False)
        taken = jnp.sum(jnp.where(rank_ref[:, cols] < PEER_TOPK, 1.0, 0.0))

        @pl.when(taken != float(PEER_TOPK * LANES))
        def _():
            run(cols, exact=True)


def _peer_query_kernel(x_ref, mod_ref, g_ref, wq_ref, k1_ref, k2_ref,
                       h_ref, a1_ref, c1_ref, b1_ref, r2_ref,
                       q_ref, s_ref, r_ref, v_ref, cand_ref, crank_ref, cval_ref, *, heads, nk):
    topk = PEER_TOPK
    mod = mod_ref[0]
    h = _modulate(x_ref[0], g_ref[...], mod[3:4], mod[4:5]).astype(BF16)
    h_ref[0] = h
    q_ref[...] = jnp.dot(h, wq_ref[...], preferred_element_type=F32).astype(BF16)
    tp = q_ref.shape[0]
    pairs = [(i, topk // (i + 1)) for i in range(topk)]
    ncand = sum(n for _, n in pairs)
    ncand_pad = cand_ref.shape[0]

    for hd in range(heads):
        for half, keys_ref in enumerate((k1_ref, k2_ref)):
            qh = q_ref[:, (2 * hd + half) * nk:(2 * hd + half + 1) * nk]
            s_ref[half] = lax.dot_general(keys_ref[...], qh, NT_DIMS, preferred_element_type=F32)
            _rank_top(s_ref.at[half], r_ref.at[half], v_ref.at[half])
        off = 0
        for i, n in pairs:
            cand_ref[off:off + n, :] = v_ref[0, i:i + 1, :] + v_ref[1, 0:n, :]
            off += n
        cand_ref[ncand:ncand_pad, :] = jnp.full((ncand_pad - ncand, tp), -jnp.inf, F32)
        _rank_top(cand_ref, crank_ref, cval_ref)
        cand = cand_ref[...]
        picked = jnp.where(crank_ref[...] < topk, 1.0, 0.0)
        z = jnp.sum(picked * jnp.exp(cand - cand[0:1]), axis=0, keepdims=True)
        r1 = r_ref[0]
        r2 = r_ref[1]
        a1 = jnp.where(r1 < topk, jnp.exp(s_ref[0] - v_ref[0, 0:1, :]), 0.0) / z
        cnt1 = jnp.zeros((nk, tp), F32)
        off = 0
        for i, n in pairs:
            c_i = jnp.sum(picked[off:off + n], axis=0, keepdims=True)
            cnt1 = jnp.where(r1 == float(i), c_i, cnt1)
            off += n
        a1_ref[hd] = a1
        c1_ref[hd] = cnt1
        b1_ref[hd] = jnp.where(r2 < topk, jnp.exp(s_ref[1] - v_ref[1, 0:1, :]), 0.0).astype(BF16)
        r2_ref[hd] = r2.astype(BF16)


def _peer_query(x, mod, norm_g, wq, k1, k2, tp=256):
    bsz, s, d = x.shape
    nk = k1.shape[0]
    heads = wq.shape[1] // (2 * k1.shape[1])
    assert heads == PEER_HEADS and k1.shape[1] == nk
    ntok = bsz * s
    nt = s // tp
    full = lambda *shape: pl.BlockSpec(shape, lambda b, i: (0,) * len(shape))
    gate_spec = pl.BlockSpec((heads, nk, tp), lambda b, i: (0, 0, b * nt + i))
    gate_shape = jax.ShapeDtypeStruct((heads, nk, ntok), F32)
    ncand_pad = 56
    return pl.pallas_call(
        functools.partial(_peer_query_kernel, heads=heads, nk=nk),
        grid=(bsz, nt),
        in_specs=[pl.BlockSpec((1, tp, d), lambda b, i: (b, i, 0)),
                  pl.BlockSpec((1, 6, d), lambda b, i: (b, 0, 0)),
                  full(1, d), full(d, wq.shape[1]), full(nk, nk), full(nk, nk)],
        out_specs=[pl.BlockSpec((1, tp, d), lambda b, i: (b, i, 0)), gate_spec, gate_spec, gate_spec, gate_spec],
        out_shape=[jax.ShapeDtypeStruct((bsz, s, d), BF16), gate_shape, gate_shape,
                   jax.ShapeDtypeStruct(gate_shape.shape, BF16), jax.ShapeDtypeStruct(gate_shape.shape, BF16)],
        scratch_shapes=[pltpu.VMEM((tp, wq.shape[1]), BF16), pltpu.VMEM((2, nk, tp), F32),
                        pltpu.VMEM((2, nk, tp), F32), pltpu.VMEM((2, PEER_TOPK, tp), F32),
                        pltpu.VMEM((ncand_pad, tp), F32), pltpu.VMEM((ncand_pad, tp), F32),
                        pltpu.VMEM((PEER_TOPK, tp), F32)],
        compiler_params=_params("arbitrary", "arbitrary"),
        name="peer_query",
    )(x, mod, norm_g.reshape(1, d), wq.astype(BF16), k1.astype(BF16), k2.astype(BF16))


def _peer_expert_kernel(h_ref, u_ref, vt_ref, a1_ref, c1_ref, b1_ref, r2_ref, x_ref, mod_ref, o_ref,
                        s_ref, w_ref, acc_ref, *, heads, nk, n_e1, tl):
    j = pl.program_id(1)

    @pl.when(j == 0)
    def _():
        acc_ref[...] = jnp.zeros_like(acc_ref)

    tt = h_ref.shape[0]
    pack = 2 * SUBLANES
    zero = jnp.zeros((nk, LANES), BF16)

    def bcast_rows(ref, hd, e, cols):
        return ref[hd, e:e + 1, cols].astype(BF16)

    for t0 in range(0, tt, tl):
        tcols = slice(t0, t0 + tl)
        s_ref[:, tcols] = lax.dot_general(u_ref[...], h_ref[tcols, :], NT_DIMS,
                                          preferred_element_type=F32)
        for e in range(n_e1):
            rows = slice(e * nk, (e + 1) * nk)
            for c0 in range(t0, t0 + tl, LANES):
                cols = slice(c0, c0 + LANES)
                s = s_ref[rows, cols]
                act = (0.5 * s * (1.0 + lax.erf(s * (1.0 / math.sqrt(2.0))))).astype(BF16)
                gate = zero
                for hd in range(heads):
                    cnt = bcast_rows(c1_ref, hd, e, cols)
                    a1 = bcast_rows(a1_ref, hd, e, cols)
                    gate = gate + jnp.where(r2_ref[hd, :, cols] < cnt, b1_ref[hd, :, cols], zero) * a1
                w_ref[rows, cols] = act * gate
        acc_ref[:, tcols] += jnp.dot(vt_ref[...], w_ref[:, tcols], preferred_element_type=F32)

    @pl.when(j == pl.num_programs(1) - 1)
    def _():
        o_ref[...] = x_ref[...] + (1.0 + mod_ref[0][5:6]) * acc_ref[...].T


def _peer_experts(h2, a1, c1, b1, r2, x, mod, expert_u, expert_v, tt=512, ec=1024, tl=256):
    bsz, s, d = x.shape
    ntok = bsz * s
    heads, nk, _ = a1.shape
    n_exp = expert_u.shape[0]
    n_e1 = ec // nk
    tiles_per_seq = s // tt
    u = expert_u.astype(BF16)
    vt = expert_v.T.astype(BF16)
    out = pl.pallas_call(
        functools.partial(_peer_expert_kernel, heads=heads, nk=nk, n_e1=n_e1, tl=tl),
        grid=(ntok // tt, n_exp // ec),
        in_specs=[pl.BlockSpec((tt, d), lambda i, j: (i, 0)),
                  pl.BlockSpec((ec, d), lambda i, j: (j, 0)),
                  pl.BlockSpec((d, ec), lambda i, j: (0, j)),
                  pl.BlockSpec((heads, n_e1, tt), lambda i, j: (0, j, i)),
                  pl.BlockSpec((heads, n_e1, tt), lambda i, j: (0, j, i)),
                  pl.BlockSpec((heads, nk, tt), lambda i, j: (0, 0, i)),
                  pl.BlockSpec((heads, nk, tt), lambda i, j: (0, 0, i)),
                  pl.BlockSpec((tt, d), lambda i, j: (i, 0)),
                  pl.BlockSpec((1, 6, d), lambda i, j: (i // tiles_per_seq, 0, 0))],
        out_specs=pl.BlockSpec((tt, d), lambda i, j: (i, 0)),
        out_shape=jax.ShapeDtypeStruct((ntok, d), F32),
        scratch_shapes=[pltpu.VMEM((ec, tt), F32), pltpu.VMEM((ec, tt), BF16), pltpu.VMEM((d, tt), F32)],
        compiler_params=_params("arbitrary", "arbitrary"),
        name="peer_experts",
    )(h2.reshape(ntok, d), u, vt, a1, c1, b1, r2, x.reshape(ntok, d), mod)
    return out.reshape(bsz, s, d)


def _peer_ffn(x, mod, norm_g, wq, k1, k2, expert_u, expert_v):
    h2, a1, c1, b1, r2 = _peer_query(x, mod, norm_g, wq, k1, k2)
    return _peer_experts(h2, a1, c1, b1, r2, x, mod, expert_u, expert_v)


def kernel(x, c, l0_ada_w, l0_ada_b, l0_norm_mix, l0_w_in, l0_b_igate, l0_b_fgate, l0_conv_w, l0_mlstm_norm, l0_w_out, l0_norm_ffn, l0_peer_wq, l0_peer_k1, l0_peer_k2, l0_peer_u, l0_peer_v, l1_ada_w, l1_ada_b, l1_norm_mix, l1_w_qkv, l1_q_norm, l1_k_norm, l1_w_out, l1_norm_ffn, l1_peer_wq, l1_peer_k1, l1_peer_k2, l1_peer_u, l1_peer_v):
    mod0 = _ada(c, l0_ada_w, l0_ada_b)
    qkvo, hc, gates_col, gates_row = _l0_in(x, mod0, l0_norm_mix, l0_w_in, l0_b_igate, l0_b_fgate, l0_conv_w)
    hm = _mlstm(qkvo, gates_col, gates_row, l0_mlstm_norm)
    width = hm.shape[2]
    w_out0 = l0_w_out.astype(BF16)
    x = _projres([hm, hc], [w_out0[:width], w_out0[width:]], x, mod0, gate_row=2)
    x = _peer_ffn(x, mod0, l0_norm_ffn, l0_peer_wq, l0_peer_k1, l0_peer_k2, l0_peer_u, l0_peer_v)
    mod1 = _ada(c, l1_ada_w, l1_ada_b)
    q, k, v = _l1_qkv(x, mod1, l1_norm_mix, l1_w_qkv, l1_q_norm, l1_k_norm)
    o = _sb_attention(q, k, v)
    x = _projres([o], [l1_w_out.astype(BF16)], x, mod1, gate_row=2)
    x = _peer_ffn(x, mod1, l1_norm_ffn, l1_peer_wq, l1_peer_k1, l1_peer_k2, l1_peer_u, l1_peer_v)
    return x
```

```python
import functools
import math

import jax
import jax.numpy as jnp
from jax import lax
from jax.experimental import pallas as pl
from jax.experimental.pallas import tpu as pltpu

F32 = jnp.float32
BF16 = jnp.bfloat16
EPS = 1e-6
NT_DIMS = (((1,), (1,)), ((), ()))

LANES = 128
SUBLANES = 8
VMEM_LIMIT_BYTES = 56 * 1024 * 1024

MLSTM_HEADS = 4
MLSTM_CHUNK = 128
SB_HEADS = 16
PEER_HEADS = 8
PEER_TOPK = 16
ATT_LOG_CUTOFF = 100.0


def _params(*semantics):
    return pltpu.CompilerParams(dimension_semantics=semantics, vmem_limit_bytes=VMEM_LIMIT_BYTES)


def _modulate(x, g, shift, scale):
    ms = jnp.mean(x * x, axis=-1, keepdims=True)
    return x * lax.rsqrt(ms + EPS) * g * (1.0 + scale) + shift


def _log_sigmoid(x):
    return jnp.minimum(x, 0.0) - jnp.log1p(jnp.exp(-jnp.abs(x)))


def _ada_kernel(c_ref, w_ref, b_ref, o_ref):
    c = c_ref[...]
    o_ref[...] = jnp.dot(c * jax.nn.sigmoid(c), w_ref[...], preferred_element_type=F32,
                         precision=lax.Precision.HIGHEST) + b_ref[...]


def _ada(c, w, b):
    bsz, d = c.shape
    n = w.shape[1]
    tn = n // 4
    mod = pl.pallas_call(
        _ada_kernel,
        grid=(n // tn,),
        in_specs=[pl.BlockSpec((bsz, d), lambda j: (0, 0)),
                  pl.BlockSpec((d, tn), lambda j: (0, j)),
                  pl.BlockSpec((1, tn), lambda j: (0, j))],
        out_specs=pl.BlockSpec((bsz, tn), lambda j: (0, j)),
        out_shape=jax.ShapeDtypeStruct((bsz, n), F32),
        compiler_params=_params("arbitrary"),
        name="ada",
    )(c, w, b.reshape(1, n))
    return mod.reshape(bsz, 6, d)


def _l0_in_kernel(x_ref, mod_ref, g_ref, wa_ref, wc_ref, wg_ref, wgt_ref, bgc_ref, bgr_ref, cw_ref,
                  qkvo_ref, hc_ref, gc_ref, gr_ref, carry_ref, *, tm, width):
    @pl.when(pl.program_id(1) == 0)
    def _():
        carry_ref[...] = jnp.zeros_like(carry_ref)

    mod = mod_ref[0]
    h = _modulate(x_ref[0], g_ref[...], mod[0:1], mod[1:2]).astype(BF16)
    qkvo_ref[0] = jnp.dot(h, wa_ref[...], preferred_element_type=F32).astype(BF16)
    gc_ref[0] = jnp.dot(h, wg_ref[...], preferred_element_type=F32) + bgc_ref[...]
    gr_ref[0] = lax.dot_general(wgt_ref[...], h, NT_DIMS, preferred_element_type=F32) + bgr_ref[...]
    pc = jnp.dot(h, wc_ref[...], preferred_element_type=F32)
    cb, cc, ch = pc[:, :width], pc[:, width:2 * width], pc[:, 2 * width:]
    u = cc * ch
    prev = carry_ref[...]
    row = lax.broadcasted_iota(jnp.int32, u.shape, 0)
    u1 = jnp.where(row == 0, prev[SUBLANES - 1:SUBLANES], pltpu.roll(u, 1, 0))
    u2 = jnp.where(row == 0, prev[SUBLANES - 2:SUBLANES - 1],
                   jnp.where(row == 1, prev[SUBLANES - 1:SUBLANES], pltpu.roll(u, 2, 0)))
    cw = cw_ref[...]
    conv = cw[0:1] * u2 + cw[1:2] * u1 + cw[2:3] * u
    hc_ref[0] = (cb * conv).astype(BF16)
    carry_ref[...] = u[tm - SUBLANES:tm]


def _l0_in(x, mod, norm_g, w_in, b_igate, b_fgate, conv_w, tm=512):
    bsz, s, d = x.shape
    hn = MLSTM_HEADS
    width = conv_w.shape[1]
    wa = w_in[:, :4 * width].astype(BF16)
    wg_f = w_in[:, 4 * width:4 * width + 2 * hn]
    wc = w_in[:, 4 * width + 2 * hn:].astype(BF16)
    wg = jnp.pad(wg_f, ((0, 0), (0, LANES - 2 * hn))).astype(BF16)
    wgt = wg_f.T.astype(BF16)
    bias = jnp.concatenate([b_igate, b_fgate])
    bgc = jnp.pad(bias, (0, LANES - 2 * hn)).reshape(1, LANES)
    bgr = bias.reshape(2 * hn, 1)
    full = lambda *shape: pl.BlockSpec(shape, lambda b, i: (0,) * len(shape))
    return pl.pallas_call(
        functools.partial(_l0_in_kernel, tm=tm, width=width),
        grid=(bsz, s // tm),
        in_specs=[pl.BlockSpec((1, tm, d), lambda b, i: (b, i, 0)),
                  pl.BlockSpec((1, 6, d), lambda b, i: (b, 0, 0)),
                  full(1, d), full(d, 4 * width), full(d, 3 * width), full(d, LANES), full(2 * hn, d),
                  full(1, LANES), full(2 * hn, 1), full(3, width)],
        out_specs=[pl.BlockSpec((1, tm, 4 * width), lambda b, i: (b, i, 0)),
                   pl.BlockSpec((1, tm, width), lambda b, i: (b, i, 0)),
                   pl.BlockSpec((1, tm, LANES), lambda b, i: (b, i, 0)),
                   pl.BlockSpec((1, 2 * hn, tm), lambda b, i: (b, 0, i))],
        out_shape=[jax.ShapeDtypeStruct((bsz, s, 4 * width), BF16),
                   jax.ShapeDtypeStruct((bsz, s, width), BF16),
                   jax.ShapeDtypeStruct((bsz, s, LANES), F32),
                   jax.ShapeDtypeStruct((bsz, 2 * hn, s), F32)],
        scratch_shapes=[pltpu.VMEM((SUBLANES, width), F32)],
        compiler_params=_params("arbitrary", "arbitrary"),
        name="l0_in",
    )(x, mod, norm_g.reshape(1, d), wa, wc, wg, wgt, bgc, bgr, conv_w)


def _mlstm_kernel(qkvo_ref, gc_ref, gr_ref, nrm_ref, hm_ref, s_ref, n_ref, m_ref, *, hn, dh, chunk, nchunk):
    @pl.when(pl.program_id(1) == 0)
    def _():
        s_ref[...] = jnp.zeros_like(s_ref)
        n_ref[...] = jnp.zeros_like(n_ref)
        m_ref[...] = jnp.zeros_like(m_ref)

    width = hn * dh
    scale = dh ** -0.5
    ri = lax.broadcasted_iota(jnp.int32, (chunk, chunk), 0)
    ci = lax.broadcasted_iota(jnp.int32, (chunk, chunk), 1)
    causal = ci <= ri
    for c in range(nchunk):
        r0 = c * chunk
        gcol = gc_ref[0, r0:r0 + chunk, :]
        grow = gr_ref[0, :, r0:r0 + chunk]
        for h in range(hn):
            q = qkvo_ref[0, r0:r0 + chunk, h * dh:(h + 1) * dh]
            k = qkvo_ref[0, r0:r0 + chunk, width + h * dh:width + (h + 1) * dh]
            v = qkvo_ref[0, r0:r0 + chunk, 2 * width + h * dh:2 * width + (h + 1) * dh]
            o = qkvo_ref[0, r0:r0 + chunk, 3 * width + h * dh:3 * width + (h + 1) * dh]
            li_col = gcol[:, h:h + 1]
            lf_col = _log_sigmoid(gcol[:, hn + h:hn + h + 1])
            li_row = grow[h:h + 1, :]
            lf_row = _log_sigmoid(grow[hn + h:hn + h + 1, :])
            b_col = jnp.sum(jnp.where(causal, lf_row, 0.0), axis=1, keepdims=True)
            b_row = jnp.sum(jnp.where(ri <= ci, lf_col, 0.0), axis=0, keepdims=True)
            g = b_col[chunk - 1:chunk, :]
            a_col = g - b_col + li_col
            s_prev = s_ref[h]
            n_prev = n_ref[h]
            m_prev = m_ref[h]
            log_d = jnp.where(causal, b_col - b_row + li_row, -jnp.inf)
            inter_log = b_col + m_prev
            m_t = jnp.maximum(inter_log, jnp.max(log_d, axis=1, keepdims=True))
            qk = lax.dot_general(q, k, NT_DIMS, preferred_element_type=F32) * scale * jnp.exp(log_d - m_t)
            inter_w = jnp.exp(inter_log - m_t) * scale
            num = (jnp.dot(qk.astype(BF16), v, preferred_element_type=F32)
                   + inter_w * jnp.dot(q, s_prev.astype(BF16), preferred_element_type=F32))
            den = (jnp.sum(qk, axis=1, keepdims=True)
                   + inter_w * jnp.sum(q.astype(F32) * n_prev, axis=1, keepdims=True))
            hv = num / jnp.maximum(jnp.abs(den), jnp.exp(-m_t))
            hv = hv * lax.rsqrt(jnp.mean(hv * hv, axis=-1, keepdims=True) + EPS) * nrm_ref[:, h * dh:(h + 1) * dh]
            hm_ref[0, r0:r0 + chunk, h * dh:(h + 1) * dh] = (hv * jax.nn.sigmoid(o.astype(F32))).astype(BF16)
            m_new = jnp.maximum(g + m_prev, jnp.max(a_col, axis=0, keepdims=True))
            decay = jnp.exp(g + m_prev - m_new)
            kw = k.astype(F32) * jnp.exp(a_col - m_new)
            s_ref[h] = decay * s_prev + jnp.dot(kw.T.astype(BF16), v, preferred_element_type=F32)
            n_ref[h] = decay * n_prev + jnp.sum(kw, axis=0, keepdims=True)
            m_ref[h] = m_new


def _mlstm(qkvo, gates_col, gates_row, mlstm_norm, tq=512):
    bsz, s, w4 = qkvo.shape
    hn = MLSTM_HEADS
    width = w4 // 4
    dh = width // hn
    return pl.pallas_call(
        functools.partial(_mlstm_kernel, hn=hn, dh=dh, chunk=MLSTM_CHUNK, nchunk=tq // MLSTM_CHUNK),
        grid=(bsz, s // tq),
        in_specs=[pl.BlockSpec((1, tq, w4), lambda b, i: (b, i, 0)),
                  pl.BlockSpec((1, tq, LANES), lambda b, i: (b, i, 0)),
                  pl.BlockSpec((1, 2 * hn, tq), lambda b, i: (b, 0, i)),
                  pl.BlockSpec((1, width), lambda b, i: (0, 0))],
        out_specs=pl.BlockSpec((1, tq, width), lambda b, i: (b, i, 0)),
        out_shape=jax.ShapeDtypeStruct((bsz, s, width), BF16),
        scratch_shapes=[pltpu.VMEM((hn, dh, dh), F32), pltpu.VMEM((hn, 1, dh), F32), pltpu.VMEM((hn, 1, 1), F32)],
        compiler_params=_params("arbitrary", "arbitrary"),
        name="mlstm",
    )(qkvo, gates_col, gates_row, mlstm_norm.reshape(1, width))


def _projres_kernel(*refs, n_act, gate_row):
    acts, ws = refs[:n_act], refs[n_act:2 * n_act]
    x_ref, mod_ref, o_ref = refs[2 * n_act:]
    y = jnp.dot(acts[0][0], ws[0][...], preferred_element_type=F32)
    for a_ref, w_ref in zip(acts[1:], ws[1:]):
        y = y + jnp.dot(a_ref[0], w_ref[...], preferred_element_type=F32)
    gate = mod_ref[0][gate_row:gate_row + 1]
    o_ref[0] = x_ref[0] + (1.0 + gate) * y


def _projres(acts, ws, x, mod, gate_row, tm=512):
    bsz, s, d = x.shape
    n_act = len(acts)
    in_specs = ([pl.BlockSpec((1, tm, a.shape[2]), lambda b, i: (b, i, 0)) for a in acts]
                + [pl.BlockSpec(w.shape, lambda b, i: (0, 0)) for w in ws]
                + [pl.BlockSpec((1, tm, d), lambda b, i: (b, i, 0)),
                   pl.BlockSpec((1, 6, d), lambda b, i: (b, 0, 0))])
    return pl.pallas_call(
        functools.partial(_projres_kernel, n_act=n_act, gate_row=gate_row),
        grid=(bsz, s // tm),
        in_specs=in_specs,
        out_specs=pl.BlockSpec((1, tm, d), lambda b, i: (b, i, 0)),
        out_shape=jax.ShapeDtypeStruct((bsz, s, d), F32),
        compiler_params=_params("arbitrary", "arbitrary"),
        name="projres",
    )(*acts, *ws, x, mod)


def _l1_qkv_kernel(x_ref, mod_ref, g_ref, w_ref, qn_ref, kn_ref, q_ref, k_ref, v_ref, *, d, dh):
    mod = mod_ref[0]
    h = _modulate(x_ref[0], g_ref[...], mod[0:1], mod[1:2]).astype(BF16)
    y = jnp.dot(h, w_ref[...], preferred_element_type=F32)
    tm = y.shape[0]
    low = lax.broadcasted_iota(jnp.int32, (tm, LANES), 1) < dh
    scale = dh ** -0.5

    def headnorm(t, gain):
        sq = t * t
        s_low = jnp.sum(jnp.where(low, sq, 0.0), axis=-1, keepdims=True)
        s_all = jnp.sum(sq, axis=-1, keepdims=True)
        ms = jnp.where(low, s_low, s_all - s_low) * (1.0 / dh)
        return t * lax.rsqrt(ms + EPS) * gain

    for p in range(d // LANES):
        sl = slice(p * LANES, (p + 1) * LANES)
        q_ref[0, :, sl] = (headnorm(y[:, sl], qn_ref[...]) * scale).astype(BF16)
        k_ref[0, :, sl] = headnorm(y[:, d + p * LANES:d + (p + 1) * LANES], kn_ref[...]).astype(BF16)
    v_ref[0] = y[:, 2 * d:].astype(BF16)


def _l1_qkv(x, mod, norm_g, w_qkv, q_norm, k_norm, tm=512):
    bsz, s, d = x.shape
    dh = q_norm.shape[0]
    assert LANES == 2 * dh
    full = lambda *shape: pl.BlockSpec(shape, lambda b, i: (0,) * len(shape))
    tok = pl.BlockSpec((1, tm, d), lambda b, i: (b, i, 0))
    return pl.pallas_call(
        functools.partial(_l1_qkv_kernel, d=d, dh=dh),
        grid=(bsz, s // tm),
        in_specs=[tok, pl.BlockSpec((1, 6, d), lambda b, i: (b, 0, 0)), full(1, d), full(d, 3 * d),
                  full(1, LANES), full(1, LANES)],
        out_specs=[tok, tok, tok],
        out_shape=[jax.ShapeDtypeStruct((bsz, s, d), BF16)] * 3,
        compiler_params=_params("arbitrary", "arbitrary"),
        name="l1_qkv",
    )(x, mod, norm_g.reshape(1, d), w_qkv.astype(BF16), jnp.tile(q_norm, 2).reshape(1, LANES),
      jnp.tile(k_norm, 2).reshape(1, LANES))


def _sb_kernel(q_ref, k_ref, v_ref, o_ref, carry_ref, acc_ref, *, tq, dh):
    qi = pl.program_id(2)
    q = q_ref[0]
    lane = lax.broadcasted_iota(jnp.int32, (tq, LANES), 1)
    ri = lax.broadcasted_iota(jnp.int32, (tq, tq), 0)
    ci = lax.broadcasted_iota(jnp.int32, (tq, tq), 1)
    strict = ci < ri
    rhs = jnp.concatenate([jnp.where(ri > ci, 1.0, 0.0), jnp.ones((tq, LANES), F32)], axis=1).astype(BF16)
    rep = tq // LANES
    n_heads = LANES // dh
    hmasks = [(lane >= hh * dh) & (lane < (hh + 1) * dh) for hh in range(n_heads)]
    qms = [jnp.where(m, q, jnp.zeros_like(q)) for m in hmasks]

    def block(kb, diag):
        start = pl.multiple_of(kb * tq, tq)
        kblk = k_ref[0, pl.ds(start, tq), :]
        vblk = v_ref[0, pl.ds(start, tq), :]
        worst = None
        for hh in range(n_heads):
            z = lax.dot_general(qms[hh], kblk, NT_DIMS, preferred_element_type=F32)
            lsn = _log_sigmoid(-z)
            if diag:
                lsn = jnp.where(strict, lsn, 0.0)
            hi = lsn.astype(BF16)
            lo = (lsn - hi.astype(F32)).astype(BF16)
            rr = jnp.dot(hi, rhs, preferred_element_type=F32) + jnp.dot(lo, rhs, preferred_element_type=F32)
            logit = z + lsn + rr[:, :tq]
            if not diag:
                logit = logit + jnp.concatenate([carry_ref[hh]] * rep, axis=1)
            p = jnp.exp(logit)
            if diag:
                p = jnp.where(strict, p, 0.0)
            pv = jnp.dot(p.astype(BF16), vblk, preferred_element_type=F32)
            if diag:
                acc_ref[hh] = pv
                carry = rr[:, tq:]
            else:
                acc_ref[hh] += pv
                carry = carry_ref[hh] + rr[:, tq:]
            carry_ref[hh] = carry
            top = jnp.max(carry)
            worst = top if worst is None else jnp.maximum(worst, top)
        return worst

    def cond(state):
        it, worst = state
        return (it < qi) & (worst > -ATT_LOG_CUTOFF)

    def body(state):
        it, _ = state
        return it + 1, block(qi - 1 - it, False)

    lax.while_loop(cond, body, (jnp.int32(0), block(qi, True)))
    out = acc_ref[0]
    for hh in range(1, n_heads):
        out = jnp.where(hmasks[hh], acc_ref[hh], out)
    o_ref[0] = out.astype(BF16)


def _sb_attention(q, k, v, tq=256):
    bsz, s, d = q.shape
    dh = d // SB_HEADS
    npair = d // LANES
    return pl.pallas_call(
        functools.partial(_sb_kernel, tq=tq, dh=dh),
        grid=(bsz, npair, s // tq),
        in_specs=[pl.BlockSpec((1, tq, LANES), lambda b, p, i: (b, i, p)),
                  pl.BlockSpec((1, s, LANES), lambda b, p, i: (b, 0, p)),
                  pl.BlockSpec((1, s, LANES), lambda b, p, i: (b, 0, p))],
        out_specs=pl.BlockSpec((1, tq, LANES), lambda b, p, i: (b, i, p)),
        out_shape=jax.ShapeDtypeStruct((bsz, s, d), BF16),
        scratch_shapes=[pltpu.VMEM((LANES // dh, tq, LANES), F32), pltpu.VMEM((LANES // dh, tq, LANES), F32)],
        compiler_params=_params("arbitrary", "arbitrary", "arbitrary"),
        name="sb_attn",
    )(q, k, v)


def _top_values(s_ref, vals_ref, n_pad=0):
    n, width = s_ref.shape
    slot = lax.broadcasted_iota(jnp.int32, (PEER_TOPK, LANES), 0)
    tiles = [slice(c0, c0 + LANES) for c0 in range(0, width, LANES)]

    def body(i, carry):
        out = []
        for cur, vals in carry:
            mx = jnp.max(cur, axis=0, keepdims=True)
            out.append((jnp.where(cur == mx, -jnp.inf, cur), jnp.where(slot == i, mx, vals)))
        return tuple(out)

    init = tuple((s_ref[:, cols], jnp.zeros((PEER_TOPK, LANES), F32)) for cols in tiles)
    dropped = 0.0
    for cols, (cur, vals) in zip(tiles, lax.fori_loop(0, PEER_TOPK, body, init)):
        vals_ref[:, cols] = vals
        dropped = dropped + jnp.sum(jnp.where(cur == -jnp.inf, 1.0, 0.0))
    return dropped == float((PEER_TOPK + n_pad) * width)


def _rank_top(s_ref, rank_ref, vals_ref):
    n, width = s_ref.shape
    rowf = lax.broadcasted_iota(jnp.int32, (n, LANES), 0).astype(F32)
    slot = lax.broadcasted_iota(jnp.int32, (PEER_TOPK, LANES), 0)

    def body(i, carry):
        cur, rank, vals = carry
        mx = jnp.max(cur, axis=0, keepdims=True)
        hit = rowf == jnp.min(jnp.where(cur == mx, rowf, float(n)), axis=0, keepdims=True)
        return (jnp.where(hit, -jnp.inf, cur), jnp.where(hit, jnp.asarray(i, F32), rank),
                jnp.where(slot == i, mx, vals))

    for c0 in range(0, width, LANES):
        cols = slice(c0, c0 + LANES)
        init = (s_ref[:, cols], jnp.full((n, LANES), float(PEER_TOPK), F32), jnp.zeros((PEER_TOPK, LANES), F32))
        _, rank, vals = lax.fori_loop(0, PEER_TOPK, body, init)
        rank_ref[:, cols] = rank
        vals_ref[:, cols] = vals


def _twice_bf16(x):
    bits = pltpu.bitcast(x.astype(BF16).astype(F32), jnp.uint32)
    return bits | (bits >> 16)


def _peer_query_kernel(x_ref, mod_ref, g_ref, wq_ref, k1_ref, k2_ref,
                       h_ref, a1_ref, c1_ref, b1_ref, r2_ref,
                       q_ref, s_ref, r_ref, v_ref, cand_ref, crank_ref, cval_ref, *, heads, nk):
    topk = PEER_TOPK
    mod = mod_ref[0]
    h = _modulate(x_ref[0], g_ref[...], mod[3:4], mod[4:5]).astype(BF16)
    h_ref[0] = h
    q_ref[...] = jnp.dot(h, wq_ref[...], preferred_element_type=F32).astype(BF16)
    tp = q_ref.shape[0]
    pairs = [(i, topk // (i + 1)) for i in range(topk)]
    ncand = sum(n for _, n in pairs)
    ncand_pad = cand_ref.shape[0]

    for hd in range(heads):
        for half, keys_ref in enumerate((k1_ref, k2_ref)):
            qh = q_ref[:, (2 * hd + half) * nk:(2 * hd + half + 1) * nk]
            s_ref[2 * hd + half] = lax.dot_general(keys_ref[...], qh, NT_DIMS, preferred_element_type=F32)

    def build_candidates():
        off = 0
        for i, n in pairs:
            cand_ref[off:off + n, :] = v_ref[0, i:i + 1, :] + v_ref[1, 0:n, :]
            off += n
        cand_ref[ncand:ncand_pad, :] = jnp.full((ncand_pad - ncand, tp), -jnp.inf, F32)

    def emit(hd, a1, cnt1, b1, r2):
        a1_ref[hd] = _twice_bf16(a1)
        c1_ref[hd] = _twice_bf16(cnt1)
        b1w = pltpu.bitcast(b1.astype(BF16), jnp.uint32)
        r2w = pltpu.bitcast(r2.astype(BF16), jnp.uint32)
        for lt in range(tp // LANES):
            b1_ref[lt, hd] = b1w[:, lt * LANES:(lt + 1) * LANES]
            r2_ref[lt, hd] = r2w[:, lt * LANES:(lt + 1) * LANES]

    def softmax_norm(picked):
        cand = cand_ref[...]
        return jnp.sum(picked * jnp.exp(cand - cand[0:1]), axis=0, keepdims=True)

    def pair_counts(picked):
        counts, off = [], 0
        for _, n in pairs:
            counts.append(jnp.sum(picked[off:off + n], axis=0, keepdims=True))
            off += n
        return counts

    for hd in range(heads):
        s1_ref, s2_ref = s_ref.at[2 * hd], s_ref.at[2 * hd + 1]
        distinct = _top_values(s1_ref, v_ref.at[0]) & _top_values(s2_ref, v_ref.at[1])
        build_candidates()
        distinct = distinct & _top_values(cand_ref, cval_ref, n_pad=ncand_pad - ncand)
        picked = jnp.where(cand_ref[...] >= cval_ref[topk - 1:topk, :], 1.0, 0.0)
        z = softmax_norm(picked)
        counts = pair_counts(picked) + [jnp.zeros((1, tp), F32)]
        for lt in range(tp // LANES):
            cols = slice(lt * LANES, (lt + 1) * LANES)
            s1 = s1_ref[:, cols]
            cnt1 = jnp.broadcast_to(counts[0][:, cols], (nk, LANES))
            for i in range(topk):
                cnt1 = jnp.where(s1 < v_ref[0, i:i + 1, cols], counts[i + 1][:, cols], cnt1)
            a1 = jnp.where(s1 >= v_ref[0, topk - 1:topk, cols], jnp.exp(s1 - v_ref[0, 0:1, cols]), 0.0) / z[:, cols]
            a1_ref[hd, :, cols] = _twice_bf16(a1)
            c1_ref[hd, :, cols] = _twice_bf16(cnt1)
            s2 = s2_ref[:, cols]
            r2 = jnp.zeros((nk, LANES), F32)
            for i in range(topk):
                r2 = jnp.where(s2 < v_ref[1, i:i + 1, cols], float(i + 1), r2)
            b1 = jnp.where(s2 >= v_ref[1, topk - 1:topk, cols], jnp.exp(s2 - v_ref[1, 0:1, cols]), 0.0)
            b1_ref[lt, hd] = pltpu.bitcast(b1.astype(BF16), jnp.uint32)
            r2_ref[lt, hd] = pltpu.bitcast(r2.astype(BF16), jnp.uint32)

        @pl.when(jnp.logical_not(distinct))
        def _():
            _rank_top(s1_ref, r_ref.at[0], v_ref.at[0])
            _rank_top(s2_ref, r_ref.at[1], v_ref.at[1])
            build_candidates()
            _rank_top(cand_ref, crank_ref, cval_ref)
            picked = jnp.where(crank_ref[...] < topk, 1.0, 0.0)
            z = softmax_norm(picked)
            r1, r2 = r_ref[0], r_ref[1]
            cnt1 = jnp.zeros((nk, tp), F32)
            for i, c_i in enumerate(pair_counts(picked)):
                cnt1 = jnp.where(r1 == float(i), c_i, cnt1)
            a1 = jnp.where(r1 < topk, jnp.exp(s1_ref[...] - v_ref[0, 0:1, :]), 0.0) / z
            b1 = jnp.where(r2 < topk, jnp.exp(s2_ref[...] - v_ref[1, 0:1, :]), 0.0)
            emit(hd, a1, cnt1, b1, r2)


def _peer_query(x, mod, norm_g, wq, k1, k2, tp=256):
    bsz, s, d = x.shape
    nk = k1.shape[0]
    heads = wq.shape[1] // (2 * k1.shape[1])
    assert heads == PEER_HEADS and k1.shape[1] == nk
    ntok = bsz * s
    nt = s // tp
    full = lambda *shape: pl.BlockSpec(shape, lambda b, i: (0,) * len(shape))
    gate_spec = pl.BlockSpec((heads, nk, tp), lambda b, i: (0, 0, b * nt + i))
    gate_shape = jax.ShapeDtypeStruct((heads, nk, ntok), jnp.uint32)
    pair_spec = pl.BlockSpec((tp // LANES, heads, nk // 2, LANES), lambda b, i: (b * nt + i, 0, 0, 0))
    pair_shape = jax.ShapeDtypeStruct((ntok // LANES, heads, nk // 2, LANES), jnp.uint32)
    ncand_pad = 56
    return pl.pallas_call(
        functools.partial(_peer_query_kernel, heads=heads, nk=nk),
        grid=(bsz, nt),
        in_specs=[pl.BlockSpec((1, tp, d), lambda b, i: (b, i, 0)),
                  pl.BlockSpec((1, 6, d), lambda b, i: (b, 0, 0)),
                  full(1, d), full(d, wq.shape[1]), full(nk, nk), full(nk, nk)],
        out_specs=[pl.BlockSpec((1, tp, d), lambda b, i: (b, i, 0)), gate_spec, gate_spec, pair_spec, pair_spec],
        out_shape=[jax.ShapeDtypeStruct((bsz, s, d), BF16), gate_shape, gate_shape, pair_shape, pair_shape],
        scratch_shapes=[pltpu.VMEM((tp, wq.shape[1]), BF16), pltpu.VMEM((2 * heads, nk, tp), F32),
                        pltpu.VMEM((2, nk, tp), F32), pltpu.VMEM((2, PEER_TOPK, tp), F32),
                        pltpu.VMEM((ncand_pad, tp), F32), pltpu.VMEM((ncand_pad, tp), F32),
                        pltpu.VMEM((PEER_TOPK, tp), F32)],
        compiler_params=_params("arbitrary", "arbitrary"),
        name="peer_query",
    )(x, mod, norm_g.reshape(1, d), wq.astype(BF16), k1.astype(BF16), k2.astype(BF16))


def _peer_expert_kernel(h_ref, u_ref, vt_ref, a1_ref, c1_ref, b1_ref, r2_ref, x_ref, mod_ref, o_ref,
                        s_ref, w_ref, acc_ref, *, heads, nk, n_e1):
    j = pl.program_id(1)

    @pl.when(j == 0)
    def _():
        acc_ref[...] = jnp.zeros_like(acc_ref)

    tt = h_ref.shape[0]
    zero = jnp.zeros((nk, LANES), BF16)

    def bcast_rows(ref, hd, e, cols):
        return pltpu.bitcast(jnp.broadcast_to(ref[hd, e:e + 1, cols], (nk // 2, LANES)), BF16)

    s = lax.dot_general(u_ref[...], h_ref[...], NT_DIMS, preferred_element_type=F32)
    for lt in range(tt // LANES):
        s_ref[lt] = s[:, lt * LANES:(lt + 1) * LANES]

    def per_lane_tile(lt, _):
        cols = pl.ds(pl.multiple_of(lt * LANES, LANES), LANES)
        for e in range(n_e1):
            rows = slice(e * nk, (e + 1) * nk)
            s = s_ref[lt, rows, :]
            act = (0.5 * s * (1.0 + lax.erf(s * (1.0 / math.sqrt(2.0))))).astype(BF16)
            gate = zero
            for hd in range(heads):
                cnt = bcast_rows(c1_ref, hd, e, cols)
                a1 = bcast_rows(a1_ref, hd, e, cols)
                r2 = pltpu.bitcast(r2_ref[lt, hd], BF16)
                b1 = pltpu.bitcast(b1_ref[lt, hd], BF16)
                gate = gate + jnp.where(r2 < cnt, b1, zero) * a1
            w_ref[rows, cols] = act * gate
        return 0

    lax.fori_loop(0, tt // LANES, per_lane_tile, 0)
    acc_ref[...] += jnp.dot(vt_ref[...], w_ref[...], preferred_element_type=F32)

    @pl.when(j == pl.num_programs(1) - 1)
    def _():
        o_ref[...] = x_ref[...] + (1.0 + mod_ref[0][5:6]) * acc_ref[...].T


def _peer_experts(h2, a1, c1, b1, r2, x, mod, expert_u, expert_v, tt=512, ec=1024):
    bsz, s, d = x.shape
    ntok = bsz * s
    heads, nk, _ = a1.shape
    n_exp = expert_u.shape[0]
    n_e1 = ec // nk
    tiles_per_seq = s // tt
    u = expert_u.astype(BF16)
    vt = expert_v.T.astype(BF16)
    out = pl.pallas_call(
        functools.partial(_peer_expert_kernel, heads=heads, nk=nk, n_e1=n_e1),
        grid=(ntok // tt, n_exp // ec),
        in_specs=[pl.BlockSpec((tt, d), lambda i, j: (i, 0)),
                  pl.BlockSpec((ec, d), lambda i, j: (j, 0)),
                  pl.BlockSpec((d, ec), lambda i, j: (0, j)),
                  pl.BlockSpec((heads, n_e1, tt), lambda i, j: (0, j, i)),
                  pl.BlockSpec((heads, n_e1, tt), lambda i, j: (0, j, i)),
                  pl.BlockSpec((tt // LANES, heads, nk // 2, LANES), lambda i, j: (i, 0, 0, 0)),
                  pl.BlockSpec((tt // LANES, heads, nk // 2, LANES), lambda i, j: (i, 0, 0, 0)),
                  pl.BlockSpec((tt, d), lambda i, j: (i, 0)),
                  pl.BlockSpec((1, 6, d), lambda i, j: (i // tiles_per_seq, 0, 0))],
        out_specs=pl.BlockSpec((tt, d), lambda i, j: (i, 0)),
        out_shape=jax.ShapeDtypeStruct((ntok, d), F32),
        scratch_shapes=[pltpu.VMEM((tt // LANES, ec, LANES), F32), pltpu.VMEM((ec, tt), BF16),
                        pltpu.VMEM((d, tt), F32)],
        compiler_params=_params("arbitrary", "arbitrary"),
        name="peer_experts",
    )(h2.reshape(ntok, d), u, vt, a1, c1, b1, r2, x.reshape(ntok, d), mod)
    return out.reshape(bsz, s, d)


def _peer_ffn(x, mod, norm_g, wq, k1, k2, expert_u, expert_v):
    h2, a1, c1, b1, r2 = _peer_query(x, mod, norm_g, wq, k1, k2)
    return _peer_experts(h2, a1, c1, b1, r2, x, mod, expert_u, expert_v)


def kernel(x, c, l0_ada_w, l0_ada_b, l0_norm_mix, l0_w_in, l0_b_igate, l0_b_fgate, l0_conv_w, l0_mlstm_norm, l0_w_out, l0_norm_ffn, l0_peer_wq, l0_peer_k1, l0_peer_k2, l0_peer_u, l0_peer_v, l1_ada_w, l1_ada_b, l1_norm_mix, l1_w_qkv, l1_q_norm, l1_k_norm, l1_w_out, l1_norm_ffn, l1_peer_wq, l1_peer_k1, l1_peer_k2, l1_peer_u, l1_peer_v):
    mod0 = _ada(c, l0_ada_w, l0_ada_b)
    qkvo, hc, gates_col, gates_row = _l0_in(x, mod0, l0_norm_mix, l0_w_in, l0_b_igate, l0_b_fgate, l0_conv_w)
    hm = _mlstm(qkvo, gates_col, gates_row, l0_mlstm_norm)
    width = hm.shape[2]
    w_out0 = l0_w_out.astype(BF16)
    x = _projres([hm, hc], [w_out0[:width], w_out0[width:]], x, mod0, gate_row=2)
    x = _peer_ffn(x, mod0, l0_norm_ffn, l0_peer_wq, l0_peer_k1, l0_peer_k2, l0_peer_u, l0_peer_v)
    mod1 = _ada(c, l1_ada_w, l1_ada_b)
    q, k, v = _l1_qkv(x, mod1, l1_norm_mix, l1_w_qkv, l1_q_norm, l1_k_norm)
    o = _sb_attention(q, k, v)
    x = _projres([o], [l1_w_out.astype(BF16)], x, mod1, gate_row=2)
    x = _peer_ffn(x, mod1, l1_norm_ffn, l1_peer_wq, l1_peer_k1, l1_peer_k2, l1_peer_u, l1_peer_v)
    return x
```

```python
import functools
import math

import jax
import jax.numpy as jnp
from jax import lax
from jax.experimental import pallas as pl
from jax.experimental.pallas import tpu as pltpu

F32 = jnp.float32
BF16 = jnp.bfloat16
EPS = 1e-6
NT_DIMS = (((1,), (1,)), ((), ()))

LANES = 128
SUBLANES = 8
VMEM_LIMIT_BYTES = 56 * 1024 * 1024

MLSTM_HEADS = 4
MLSTM_CHUNK = 128
SB_HEADS = 16
PEER_HEADS = 8
PEER_TOPK = 16
ATT_LOG_CUTOFF = 100.0


def _params(*semantics):
    return pltpu.CompilerParams(dimension_semantics=semantics, vmem_limit_bytes=VMEM_LIMIT_BYTES)


def _modulate(x, g, shift, scale):
    ms = jnp.mean(x * x, axis=-1, keepdims=True)
    return x * lax.rsqrt(ms + EPS) * g * (1.0 + scale) + shift


def _log_sigmoid(x):
    return jnp.minimum(x, 0.0) - jnp.log1p(jnp.exp(-jnp.abs(x)))


def _ada_kernel(c_ref, w_ref, b_ref, o_ref):
    c = c_ref[...]
    o_ref[...] = jnp.dot(c * jax.nn.sigmoid(c), w_ref[...], preferred_element_type=F32,
                         precision=lax.Precision.HIGHEST) + b_ref[...]


def _ada(c, w, b):
    bsz, d = c.shape
    n = w.shape[1]
    tn = n // 4
    mod = pl.pallas_call(
        _ada_kernel,
        grid=(n // tn,),
        in_specs=[pl.BlockSpec((bsz, d), lambda j: (0, 0)),
                  pl.BlockSpec((d, tn), lambda j: (0, j)),
                  pl.BlockSpec((1, tn), lambda j: (0, j))],
        out_specs=pl.BlockSpec((bsz, tn), lambda j: (0, j)),
        out_shape=jax.ShapeDtypeStruct((bsz, n), F32),
        compiler_params=_params("arbitrary"),
        name="ada",
    )(c, w, b.reshape(1, n))
    return mod.reshape(bsz, 6, d)


def _l0_in_kernel(x_ref, mod_ref, g_ref, wa_ref, wc_ref, wg_ref, wgt_ref, bgc_ref, bgr_ref, cw_ref,
                  qkvo_ref, hc_ref, gc_ref, gr_ref, carry_ref, *, tm, width):
    @pl.when(pl.program_id(1) == 0)
    def _():
        carry_ref[...] = jnp.zeros_like(carry_ref)

    mod = mod_ref[0]
    h = _modulate(x_ref[0], g_ref[...], mod[0:1], mod[1:2]).astype(BF16)
    qkvo_ref[0] = jnp.dot(h, wa_ref[...], preferred_element_type=F32).astype(BF16)
    gc_ref[0] = jnp.dot(h, wg_ref[...], preferred_element_type=F32) + bgc_ref[...]
    gr_ref[0] = lax.dot_general(wgt_ref[...], h, NT_DIMS, preferred_element_type=F32) + bgr_ref[...]
    pc = jnp.dot(h, wc_ref[...], preferred_element_type=F32)
    cb, cc, ch = pc[:, :width], pc[:, width:2 * width], pc[:, 2 * width:]
    u = cc * ch
    prev = carry_ref[...]
    row = lax.broadcasted_iota(jnp.int32, u.shape, 0)
    u1 = jnp.where(row == 0, prev[SUBLANES - 1:SUBLANES], pltpu.roll(u, 1, 0))
    u2 = jnp.where(row == 0, prev[SUBLANES - 2:SUBLANES - 1],
                   jnp.where(row == 1, prev[SUBLANES - 1:SUBLANES], pltpu.roll(u, 2, 0)))
    cw = cw_ref[...]
    conv = cw[0:1] * u2 + cw[1:2] * u1 + cw[2:3] * u
    hc_ref[0] = (cb * conv).astype(BF16)
    carry_ref[...] = u[tm - SUBLANES:tm]


def _l0_in(x, mod, norm_g, w_in, b_igate, b_fgate, conv_w, tm=512):
    bsz, s, d = x.shape
    hn = MLSTM_HEADS
    width = conv_w.shape[1]
    wa = w_in[:, :4 * width].astype(BF16)
    wg_f = w_in[:, 4 * width:4 * width + 2 * hn]
    wc = w_in[:, 4 * width + 2 * hn:].astype(BF16)
    wg = jnp.pad(wg_f, ((0, 0), (0, LANES - 2 * hn))).astype(BF16)
    wgt = wg_f.T.astype(BF16)
    bias = jnp.concatenate([b_igate, b_fgate])
    bgc = jnp.pad(bias, (0, LANES - 2 * hn)).reshape(1, LANES)
    bgr = bias.reshape(2 * hn, 1)
    full = lambda *shape: pl.BlockSpec(shape, lambda b, i: (0,) * len(shape))
    return pl.pallas_call(
        functools.partial(_l0_in_kernel, tm=tm, width=width),
        grid=(bsz, s // tm),
        in_specs=[pl.BlockSpec((1, tm, d), lambda b, i: (b, i, 0)),
                  pl.BlockSpec((1, 6, d), lambda b, i: (b, 0, 0)),
                  full(1, d), full(d, 4 * width), full(d, 3 * width), full(d, LANES), full(2 * hn, d),
                  full(1, LANES), full(2 * hn, 1), full(3, width)],
        out_specs=[pl.BlockSpec((1, tm, 4 * width), lambda b, i: (b, i, 0)),
                   pl.BlockSpec((1, tm, width), lambda b, i: (b, i, 0)),
                   pl.BlockSpec((1, tm, LANES), lambda b, i: (b, i, 0)),
                   pl.BlockSpec((1, 2 * hn, tm), lambda b, i: (b, 0, i))],
        out_shape=[jax.ShapeDtypeStruct((bsz, s, 4 * width), BF16),
                   jax.ShapeDtypeStruct((bsz, s, width), BF16),
                   jax.ShapeDtypeStruct((bsz, s, LANES), F32),
                   jax.ShapeDtypeStruct((bsz, 2 * hn, s), F32)],
        scratch_shapes=[pltpu.VMEM((SUBLANES, width), F32)],
        compiler_params=_params("arbitrary", "arbitrary"),
        name="l0_in",
    )(x, mod, norm_g.reshape(1, d), wa, wc, wg, wgt, bgc, bgr, conv_w)


def _mlstm_kernel(qkvo_ref, gc_ref, gr_ref, nrm_ref, hm_ref, s_ref, n_ref, m_ref, *, hn, dh, chunk, nchunk):
    @pl.when(pl.program_id(1) == 0)
    def _():
        s_ref[...] = jnp.zeros_like(s_ref)
        n_ref[...] = jnp.zeros_like(n_ref)
        m_ref[...] = jnp.zeros_like(m_ref)

    width = hn * dh
    scale = dh ** -0.5
    ri = lax.broadcasted_iota(jnp.int32, (chunk, chunk), 0)
    ci = lax.broadcasted_iota(jnp.int32, (chunk, chunk), 1)
    causal = ci <= ri
    for c in range(nchunk):
        r0 = c * chunk
        gcol = gc_ref[0, r0:r0 + chunk, :]
        grow = gr_ref[0, :, r0:r0 + chunk]
        for h in range(hn):
            q = qkvo_ref[0, r0:r0 + chunk, h * dh:(h + 1) * dh]
            k = qkvo_ref[0, r0:r0 + chunk, width + h * dh:width + (h + 1) * dh]
            v = qkvo_ref[0, r0:r0 + chunk, 2 * width + h * dh:2 * width + (h + 1) * dh]
            o = qkvo_ref[0, r0:r0 + chunk, 3 * width + h * dh:3 * width + (h + 1) * dh]
            li_col = gcol[:, h:h + 1]
            lf_col = _log_sigmoid(gcol[:, hn + h:hn + h + 1])
            li_row = grow[h:h + 1, :]
            lf_row = _log_sigmoid(grow[hn + h:hn + h + 1, :])
            b_col = jnp.sum(jnp.where(causal, lf_row, 0.0), axis=1, keepdims=True)
            b_row = jnp.sum(jnp.where(ri <= ci, lf_col, 0.0), axis=0, keepdims=True)
            g = b_col[chunk - 1:chunk, :]
            a_col = g - b_col + li_col
            s_prev = s_ref[h]
            n_prev = n_ref[h]
            m_prev = m_ref[h]
            log_d = jnp.where(causal, b_col - b_row + li_row, -jnp.inf)
            inter_log = b_col + m_prev
            m_t = jnp.maximum(inter_log, jnp.max(log_d, axis=1, keepdims=True))
            qk = lax.dot_general(q, k, NT_DIMS, preferred_element_type=F32) * scale * jnp.exp(log_d - m_t)
            inter_w = jnp.exp(inter_log - m_t) * scale
            num = (jnp.dot(qk.astype(BF16), v, preferred_element_type=F32)
                   + inter_w * jnp.dot(q, s_prev.astype(BF16), preferred_element_type=F32))
            den = (jnp.sum(qk, axis=1, keepdims=True)
                   + inter_w * jnp.sum(q.astype(F32) * n_prev, axis=1, keepdims=True))
            hv = num / jnp.maximum(jnp.abs(den), jnp.exp(-m_t))
            hv = hv * lax.rsqrt(jnp.mean(hv * hv, axis=-1, keepdims=True) + EPS) * nrm_ref[:, h * dh:(h + 1) * dh]
            hm_ref[0, r0:r0 + chunk, h * dh:(h + 1) * dh] = (hv * jax.nn.sigmoid(o.astype(F32))).astype(BF16)
            m_new = jnp.maximum(g + m_prev, jnp.max(a_col, axis=0, keepdims=True))
            decay = jnp.exp(g + m_prev - m_new)
            kw = k.astype(F32) * jnp.exp(a_col - m_new)
            s_ref[h] = decay * s_prev + jnp.dot(kw.T.astype(BF16), v, preferred_element_type=F32)
            n_ref[h] = decay * n_prev + jnp.sum(kw, axis=0, keepdims=True)
            m_ref[h] = m_new


def _mlstm(qkvo, gates_col, gates_row, mlstm_norm, tq=512):
    bsz, s, w4 = qkvo.shape
    hn = MLSTM_HEADS
    width = w4 // 4
    dh = width // hn
    return pl.pallas_call(
        functools.partial(_mlstm_kernel, hn=hn, dh=dh, chunk=MLSTM_CHUNK, nchunk=tq // MLSTM_CHUNK),
        grid=(bsz, s // tq),
        in_specs=[pl.BlockSpec((1, tq, w4), lambda b, i: (b, i, 0)),
                  pl.BlockSpec((1, tq, LANES), lambda b, i: (b, i, 0)),
                  pl.BlockSpec((1, 2 * hn, tq), lambda b, i: (b, 0, i)),
                  pl.BlockSpec((1, width), lambda b, i: (0, 0))],
        out_specs=pl.BlockSpec((1, tq, width), lambda b, i: (b, i, 0)),
        out_shape=jax.ShapeDtypeStruct((bsz, s, width), BF16),
        scratch_shapes=[pltpu.VMEM((hn, dh, dh), F32), pltpu.VMEM((hn, 1, dh), F32), pltpu.VMEM((hn, 1, 1), F32)],
        compiler_params=_params("arbitrary", "arbitrary"),
        name="mlstm",
    )(qkvo, gates_col, gates_row, mlstm_norm.reshape(1, width))


def _projres_kernel(*refs, n_act, gate_row):
    acts, ws = refs[:n_act], refs[n_act:2 * n_act]
    x_ref, mod_ref, o_ref = refs[2 * n_act:]
    y = jnp.dot(acts[0][0], ws[0][...], preferred_element_type=F32)
    for a_ref, w_ref in zip(acts[1:], ws[1:]):
        y = y + jnp.dot(a_ref[0], w_ref[...], preferred_element_type=F32)
    gate = mod_ref[0][gate_row:gate_row + 1]
    o_ref[0] = x_ref[0] + (1.0 + gate) * y


def _projres(acts, ws, x, mod, gate_row, tm=512):
    bsz, s, d = x.shape
    n_act = len(acts)
    in_specs = ([pl.BlockSpec((1, tm, a.shape[2]), lambda b, i: (b, i, 0)) for a in acts]
                + [pl.BlockSpec(w.shape, lambda b, i: (0, 0)) for w in ws]
                + [pl.BlockSpec((1, tm, d), lambda b, i: (b, i, 0)),
                   pl.BlockSpec((1, 6, d), lambda b, i: (b, 0, 0))])
    return pl.pallas_call(
        functools.partial(_projres_kernel, n_act=n_act, gate_row=gate_row),
        grid=(bsz, s // tm),
        in_specs=in_specs,
        out_specs=pl.BlockSpec((1, tm, d), lambda b, i: (b, i, 0)),
        out_shape=jax.ShapeDtypeStruct((bsz, s, d), F32),
        compiler_params=_params("arbitrary", "arbitrary"),
        name="projres",
    )(*acts, *ws, x, mod)


def _l1_qkv_kernel(x_ref, mod_ref, g_ref, w_ref, qn_ref, kn_ref, q_ref, k_ref, v_ref, *, d, dh):
    mod = mod_ref[0]
    h = _modulate(x_ref[0], g_ref[...], mod[0:1], mod[1:2]).astype(BF16)
    y = jnp.dot(h, w_ref[...], preferred_element_type=F32)
    tm = y.shape[0]
    low = lax.broadcasted_iota(jnp.int32, (tm, LANES), 1) < dh
    scale = dh ** -0.5

    def headnorm(t, gain):
        sq = t * t
        s_low = jnp.sum(jnp.where(low, sq, 0.0), axis=-1, keepdims=True)
        s_all = jnp.sum(sq, axis=-1, keepdims=True)
        ms = jnp.where(low, s_low, s_all - s_low) * (1.0 / dh)
        return t * lax.rsqrt(ms + EPS) * gain

    for p in range(d // LANES):
        sl = slice(p * LANES, (p + 1) * LANES)
        q_ref[0, :, sl] = (headnorm(y[:, sl], qn_ref[...]) * scale).astype(BF16)
        k_ref[0, :, sl] = headnorm(y[:, d + p * LANES:d + (p + 1) * LANES], kn_ref[...]).astype(BF16)
    v_ref[0] = y[:, 2 * d:].astype(BF16)


def _l1_qkv(x, mod, norm_g, w_qkv, q_norm, k_norm, tm=512):
    bsz, s, d = x.shape
    dh = q_norm.shape[0]
    assert LANES == 2 * dh
    full = lambda *shape: pl.BlockSpec(shape, lambda b, i: (0,) * len(shape))
    tok = pl.BlockSpec((1, tm, d), lambda b, i: (b, i, 0))
    return pl.pallas_call(
        functools.partial(_l1_qkv_kernel, d=d, dh=dh),
        grid=(bsz, s // tm),
        in_specs=[tok, pl.BlockSpec((1, 6, d), lambda b, i: (b, 0, 0)), full(1, d), full(d, 3 * d),
                  full(1, LANES), full(1, LANES)],
        out_specs=[tok, tok, tok],
        out_shape=[jax.ShapeDtypeStruct((bsz, s, d), BF16)] * 3,
        compiler_params=_params("arbitrary", "arbitrary"),
        name="l1_qkv",
    )(x, mod, norm_g.reshape(1, d), w_qkv.astype(BF16), jnp.tile(q_norm, 2).reshape(1, LANES),
      jnp.tile(k_norm, 2).reshape(1, LANES))


def _sb_kernel(q_ref, k_ref, v_ref, o_ref, carry_ref, acc_ref, *, tq, dh):
    qi = pl.program_id(2)
    q = q_ref[0]
    lane = lax.broadcasted_iota(jnp.int32, (tq, LANES), 1)
    ri = lax.broadcasted_iota(jnp.int32, (tq, tq), 0)
    ci = lax.broadcasted_iota(jnp.int32, (tq, tq), 1)
    strict = ci < ri
    rhs = jnp.concatenate([jnp.where(ri > ci, 1.0, 0.0), jnp.ones((tq, LANES), F32)], axis=1).astype(BF16)
    rep = tq // LANES
    n_heads = LANES // dh
    hmasks = [(lane >= hh * dh) & (lane < (hh + 1) * dh) for hh in range(n_heads)]
    qms = [jnp.where(m, q, jnp.zeros_like(q)) for m in hmasks]

    def block(kb, diag):
        start = pl.multiple_of(kb * tq, tq)
        kblk = k_ref[0, pl.ds(start, tq), :]
        vblk = v_ref[0, pl.ds(start, tq), :]
        worst = None
        for hh in range(n_heads):
            z = lax.dot_general(qms[hh], kblk, NT_DIMS, preferred_element_type=F32)
            lsn = _log_sigmoid(-z)
            if diag:
                lsn = jnp.where(strict, lsn, 0.0)
            hi = lsn.astype(BF16)
            lo = (lsn - hi.astype(F32)).astype(BF16)
            rr = jnp.dot(hi, rhs, preferred_element_type=F32) + jnp.dot(lo, rhs, preferred_element_type=F32)
            logit = z + lsn + rr[:, :tq]
            if not diag:
                logit = logit + jnp.concatenate([carry_ref[hh]] * rep, axis=1)
            p = jnp.exp(logit)
            if diag:
                p = jnp.where(strict, p, 0.0)
            pv = jnp.dot(p.astype(BF16), vblk, preferred_element_type=F32)
            if diag:
                acc_ref[hh] = pv
                carry = rr[:, tq:]
            else:
                acc_ref[hh] += pv
                carry = carry_ref[hh] + rr[:, tq:]
            carry_ref[hh] = carry
            top = jnp.max(carry)
            worst = top if worst is None else jnp.maximum(worst, top)
        return worst

    def cond(state):
        it, worst = state
        return (it < qi) & (worst > -ATT_LOG_CUTOFF)

    def body(state):
        it, _ = state
        return it + 1, block(qi - 1 - it, False)

    lax.while_loop(cond, body, (jnp.int32(0), block(qi, True)))
    out = acc_ref[0]
    for hh in range(1, n_heads):
        out = jnp.where(hmasks[hh], acc_ref[hh], out)
    o_ref[0] = out.astype(BF16)


def _sb_attention(q, k, v, tq=256):
    bsz, s, d = q.shape
    dh = d // SB_HEADS
    npair = d // LANES
    return pl.pallas_call(
        functools.partial(_sb_kernel, tq=tq, dh=dh),
        grid=(bsz, npair, s // tq),
        in_specs=[pl.BlockSpec((1, tq, LANES), lambda b, p, i: (b, i, p)),
                  pl.BlockSpec((1, s, LANES), lambda b, p, i: (b, 0, p)),
                  pl.BlockSpec((1, s, LANES), lambda b, p, i: (b, 0, p))],
        out_specs=pl.BlockSpec((1, tq, LANES), lambda b, p, i: (b, i, p)),
        out_shape=jax.ShapeDtypeStruct((bsz, s, d), BF16),
        scratch_shapes=[pltpu.VMEM((LANES // dh, tq, LANES), F32), pltpu.VMEM((LANES // dh, tq, LANES), F32)],
        compiler_params=_params("arbitrary", "arbitrary", "arbitrary"),
        name="sb_attn",
    )(q, k, v)


def _sorter_pairs(lo, hi):
    def merge(lo, hi, r):
        step = r * 2
        if step < hi - lo:
            yield from merge(lo, hi, step)
            yield from merge(lo + r, hi, step)
            yield from ((i, i + r) for i in range(lo + r, hi - r, step))
        else:
            yield (lo, lo + r)

    if hi - lo >= 1:
        mid = lo + (hi - lo) // 2
        yield from _sorter_pairs(lo, mid)
        yield from _sorter_pairs(mid + 1, hi)
        yield from merge(lo, hi, 1)


def _exchange(x, i, j):
    x[i], x[j] = jnp.maximum(x[i], x[j]), jnp.minimum(x[i], x[j])


def _sort_bitonic(x):
    d = len(x) // 2
    while d:
        for i in range(len(x)):
            if not i & d:
                _exchange(x, i, i + d)
        d //= 2


def _top_values(s_ref, vals_ref):
    n, width = s_ref.shape
    n_tile = n // SUBLANES
    ok_total = 0.0
    for c0 in range(0, width, LANES):
        cols = slice(c0, c0 + LANES)
        rows = [s_ref[t * SUBLANES:(t + 1) * SUBLANES, cols] for t in range(n_tile)]
        x = list(rows)
        for i, j in _sorter_pairs(0, n_tile - 1):
            _exchange(x, i, j)
        for shift in (SUBLANES // 2, SUBLANES // 4, SUBLANES // 8):
            other = [pltpu.roll(v, shift, 0) for v in x]
            if len(x) < PEER_TOPK:
                x = x + other[::-1]
            else:
                x = [jnp.maximum(a, b) for a, b in zip(x, other[::-1])]
            _sort_bitonic(x)
        for i in range(PEER_TOPK):
            vals_ref[i:i + 1, cols] = x[i][0:1]
        gap = x[0] - x[1]
        for i in range(1, PEER_TOPK - 1):
            gap = jnp.minimum(gap, x[i] - x[i + 1])
        reach = jnp.zeros((SUBLANES, LANES), F32)
        for r in rows:
            reach = reach + jnp.where(r >= x[PEER_TOPK - 1], 1.0, 0.0)
        reach = jnp.sum(reach, axis=0, keepdims=True)
        ok = jnp.where((gap[0:1] > 0.0) & (reach == float(PEER_TOPK)), 1.0, 0.0)
        ok_total = ok_total + jnp.sum(ok)
    return ok_total == float(width)


def _rank_top(s_ref, rank_ref, vals_ref):
    n, width = s_ref.shape
    rowf = lax.broadcasted_iota(jnp.int32, (n, LANES), 0).astype(F32)
    slot = lax.broadcasted_iota(jnp.int32, (PEER_TOPK, LANES), 0)

    def body(i, carry):
        cur, rank, vals = carry
        mx = jnp.max(cur, axis=0, keepdims=True)
        hit = rowf == jnp.min(jnp.where(cur == mx, rowf, float(n)), axis=0, keepdims=True)
        return (jnp.where(hit, -jnp.inf, cur), jnp.where(hit, jnp.asarray(i, F32), rank),
                jnp.where(slot == i, mx, vals))

    for c0 in range(0, width, LANES):
        cols = slice(c0, c0 + LANES)
        init = (s_ref[:, cols], jnp.full((n, LANES), float(PEER_TOPK), F32), jnp.zeros((PEER_TOPK, LANES), F32))
        _, rank, vals = lax.fori_loop(0, PEER_TOPK, body, init)
        rank_ref[:, cols] = rank
        vals_ref[:, cols] = vals


def _twice_bf16(x):
    bits = pltpu.bitcast(x.astype(BF16).astype(F32), jnp.uint32)
    return bits | (bits >> 16)


def _peer_query_kernel(x_ref, mod_ref, g_ref, wq_ref, k1_ref, k2_ref,
                       h_ref, a1_ref, c1_ref, b1_ref, r2_ref,
                       q_ref, s_ref, r_ref, v_ref, cand_ref, crank_ref, cval_ref, *, heads, nk):
    topk = PEER_TOPK
    mod = mod_ref[0]
    h = _modulate(x_ref[0], g_ref[...], mod[3:4], mod[4:5]).astype(BF16)
    h_ref[0] = h
    q_ref[...] = jnp.dot(h, wq_ref[...], preferred_element_type=F32).astype(BF16)
    tp = q_ref.shape[0]
    pairs = [(i, topk // (i + 1)) for i in range(topk)]
    ncand = sum(n for _, n in pairs)
    ncand_pad = cand_ref.shape[0]

    for hd in range(heads):
        for half, keys_ref in enumerate((k1_ref, k2_ref)):
            qh = q_ref[:, (2 * hd + half) * nk:(2 * hd + half + 1) * nk]
            s_ref[2 * hd + half] = lax.dot_general(keys_ref[...], qh, NT_DIMS, preferred_element_type=F32)

    def build_candidates():
        off = 0
        for i, n in pairs:
            cand_ref[off:off + n, :] = v_ref[0, i:i + 1, :] + v_ref[1, 0:n, :]
            off += n
        cand_ref[ncand:ncand_pad, :] = jnp.full((ncand_pad - ncand, tp), -jnp.inf, F32)

    def emit(hd, a1, cnt1, b1, r2):
        a1_ref[hd] = _twice_bf16(a1)
        c1_ref[hd] = _twice_bf16(cnt1)
        b1w = pltpu.bitcast(b1.astype(BF16), jnp.uint32)
        r2w = pltpu.bitcast(r2.astype(BF16), jnp.uint32)
        for lt in range(tp // LANES):
            b1_ref[lt, hd] = b1w[:, lt * LANES:(lt + 1) * LANES]
            r2_ref[lt, hd] = r2w[:, lt * LANES:(lt + 1) * LANES]

    def softmax_norm(picked):
        cand = cand_ref[...]
        return jnp.sum(picked * jnp.exp(cand - cand[0:1]), axis=0, keepdims=True)

    def pair_counts(picked):
        counts, off = [], 0
        for _, n in pairs:
            counts.append(jnp.sum(picked[off:off + n], axis=0, keepdims=True))
            off += n
        return counts

    for hd in range(heads):
        s1_ref, s2_ref = s_ref.at[2 * hd], s_ref.at[2 * hd + 1]
        distinct = _top_values(s1_ref, v_ref.at[0]) & _top_values(s2_ref, v_ref.at[1])
        build_candidates()
        distinct = distinct & _top_values(cand_ref, cval_ref)
        picked = jnp.where(cand_ref[...] >= cval_ref[topk - 1:topk, :], 1.0, 0.0)
        half_over_z = 0.5 / softmax_norm(picked)
        counts = pair_counts(picked) + [jnp.zeros((1, tp), F32)]
        for lt in range(tp // LANES):
            cols = slice(lt * LANES, (lt + 1) * LANES)
            s1 = s1_ref[:, cols]
            cnt1 = jnp.broadcast_to(counts[0][:, cols], (nk, LANES))
            for i in range(topk):
                cnt1 = jnp.where(s1 < v_ref[0, i:i + 1, cols], counts[i + 1][:, cols], cnt1)
            a1 = jnp.where(s1 >= v_ref[0, topk - 1:topk, cols], jnp.exp(s1 - v_ref[0, 0:1, cols]), 0.0) * half_over_z[:, cols]
            a1_ref[hd, :, cols] = _twice_bf16(a1)
            c1_ref[hd, :, cols] = _twice_bf16(cnt1)
            s2 = s2_ref[:, cols]
            r2 = jnp.zeros((nk, LANES), F32)
            for i in range(topk):
                r2 = jnp.where(s2 < v_ref[1, i:i + 1, cols], float(i + 1), r2)
            b1 = jnp.where(s2 >= v_ref[1, topk - 1:topk, cols], jnp.exp(s2 - v_ref[1, 0:1, cols]), 0.0)
            b1_ref[lt, hd] = pltpu.bitcast(b1.astype(BF16), jnp.uint32)
            r2_ref[lt, hd] = pltpu.bitcast(r2.astype(BF16), jnp.uint32)

        @pl.when(jnp.logical_not(distinct))
        def _():
            _rank_top(s1_ref, r_ref.at[0], v_ref.at[0])
            _rank_top(s2_ref, r_ref.at[1], v_ref.at[1])
            build_candidates()
            _rank_top(cand_ref, crank_ref, cval_ref)
            picked = jnp.where(crank_ref[...] < topk, 1.0, 0.0)
            z = softmax_norm(picked)
            r1, r2 = r_ref[0], r_ref[1]
            cnt1 = jnp.zeros((nk, tp), F32)
            for i, c_i in enumerate(pair_counts(picked)):
                cnt1 = jnp.where(r1 == float(i), c_i, cnt1)
            a1 = jnp.where(r1 < topk, jnp.exp(s1_ref[...] - v_ref[0, 0:1, :]), 0.0) * (0.5 / z)
            b1 = jnp.where(r2 < topk, jnp.exp(s2_ref[...] - v_ref[1, 0:1, :]), 0.0)
            emit(hd, a1, cnt1, b1, r2)


def _peer_query(x, mod, norm_g, wq, k1, k2, tp=256):
    bsz, s, d = x.shape
    nk = k1.shape[0]
    heads = wq.shape[1] // (2 * k1.shape[1])
    assert heads == PEER_HEADS and k1.shape[1] == nk
    ntok = bsz * s
    nt = s // tp
    full = lambda *shape: pl.BlockSpec(shape, lambda b, i: (0,) * len(shape))
    gate_spec = pl.BlockSpec((heads, nk, tp), lambda b, i: (0, 0, b * nt + i))
    gate_shape = jax.ShapeDtypeStruct((heads, nk, ntok), jnp.uint32)
    pair_spec = pl.BlockSpec((tp // LANES, heads, nk // 2, LANES), lambda b, i: (b * nt + i, 0, 0, 0))
    pair_shape = jax.ShapeDtypeStruct((ntok // LANES, heads, nk // 2, LANES), jnp.uint32)
    ncand_pad = 64
    return pl.pallas_call(
        functools.partial(_peer_query_kernel, heads=heads, nk=nk),
        grid=(bsz, nt),
        in_specs=[pl.BlockSpec((1, tp, d), lambda b, i: (b, i, 0)),
                  pl.BlockSpec((1, 6, d), lambda b, i: (b, 0, 0)),
                  full(1, d), full(d, wq.shape[1]), full(nk, nk), full(nk, nk)],
        out_specs=[pl.BlockSpec((1, tp, d), lambda b, i: (b, i, 0)), gate_spec, gate_spec, pair_spec, pair_spec],
        out_shape=[jax.ShapeDtypeStruct((bsz, s, d), BF16), gate_shape, gate_shape, pair_shape, pair_shape],
        scratch_shapes=[pltpu.VMEM((tp, wq.shape[1]), BF16), pltpu.VMEM((2 * heads, nk, tp), F32),
                        pltpu.VMEM((2, nk, tp), F32), pltpu.VMEM((2, PEER_TOPK, tp), F32),
                        pltpu.VMEM((ncand_pad, tp), F32), pltpu.VMEM((ncand_pad, tp), F32),
                        pltpu.VMEM((PEER_TOPK, tp), F32)],
        compiler_params=_params("arbitrary", "arbitrary"),
        name="peer_query",
    )(x, mod, norm_g.reshape(1, d), wq.astype(BF16), k1.astype(BF16), k2.astype(BF16))


def _peer_expert_kernel(h_ref, u_ref, vt_ref, a1_ref, c1_ref, b1_ref, r2_ref, x_ref, mod_ref, o_ref,
                        s_ref, w_ref, acc_ref, *, heads, nk, n_e1):
    j = pl.program_id(1)

    @pl.when(j == 0)
    def _():
        acc_ref[...] = jnp.zeros_like(acc_ref)

    tt = h_ref.shape[0]
    zero = jnp.zeros((nk, LANES), BF16)

    def bcast_rows(ref, hd, e, cols):
        return pltpu.bitcast(jnp.broadcast_to(ref[hd, e:e + 1, cols], (nk // 2, LANES)), BF16)

    s = lax.dot_general(u_ref[...], h_ref[...], NT_DIMS, preferred_element_type=F32)
    for lt in range(tt // LANES):
        s_ref[lt] = s[:, lt * LANES:(lt + 1) * LANES]

    def per_lane_tile(lt, _):
        cols = pl.ds(pl.multiple_of(lt * LANES, LANES), LANES)
        for e in range(n_e1):
            rows = slice(e * nk, (e + 1) * nk)
            s = s_ref[lt, rows, :].astype(BF16)
            act = s * (1.0 + lax.erf(s * (1.0 / math.sqrt(2.0))))
            gate = zero
            for hd in range(heads):
                cnt = bcast_rows(c1_ref, hd, e, cols)
                a1 = bcast_rows(a1_ref, hd, e, cols)
                r2 = pltpu.bitcast(r2_ref[lt, hd], BF16)
                b1 = pltpu.bitcast(b1_ref[lt, hd], BF16)
                gate = gate + jnp.where(r2 < cnt, b1, zero) * a1
            w_ref[rows, cols] = act * gate
        return 0

    lax.fori_loop(0, tt // LANES, per_lane_tile, 0)
    acc_ref[...] += jnp.dot(vt_ref[...], w_ref[...], preferred_element_type=F32)

    @pl.when(j == pl.num_programs(1) - 1)
    def _():
        o_ref[...] = x_ref[...] + (1.0 + mod_ref[0][5:6]) * acc_ref[...].T


def _peer_experts(h2, a1, c1, b1, r2, x, mod, expert_u, expert_v, tt=1024, ec=1024):
    bsz, s, d = x.shape
    ntok = bsz * s
    heads, nk, _ = a1.shape
    n_exp = expert_u.shape[0]
    n_e1 = ec // nk
    tt = min(tt, s)
    tiles_per_seq = s // tt
    u = expert_u.astype(BF16)
    vt = expert_v.T.astype(BF16)
    out = pl.pallas_call(
        functools.partial(_peer_expert_kernel, heads=heads, nk=nk, n_e1=n_e1),
        grid=(ntok // tt, n_exp // ec),
        in_specs=[pl.BlockSpec((tt, d), lambda i, j: (i, 0)),
                  pl.BlockSpec((ec, d), lambda i, j: (j, 0)),
                  pl.BlockSpec((d, ec), lambda i, j: (0, j)),
                  pl.BlockSpec((heads, n_e1, tt), lambda i, j: (0, j, i)),
                  pl.BlockSpec((heads, n_e1, tt), lambda i, j: (0, j, i)),
                  pl.BlockSpec((tt // LANES, heads, nk // 2, LANES), lambda i, j: (i, 0, 0, 0)),
                  pl.BlockSpec((tt // LANES, heads, nk // 2, LANES), lambda i, j: (i, 0, 0, 0)),
                  pl.BlockSpec((tt, d), lambda i, j: (i, 0)),
                  pl.BlockSpec((1, 6, d), lambda i, j: (i // tiles_per_seq, 0, 0))],
        out_specs=pl.BlockSpec((tt, d), lambda i, j: (i, 0)),
        out_shape=jax.ShapeDtypeStruct((ntok, d), F32),
        scratch_shapes=[pltpu.VMEM((tt // LANES, ec, LANES), F32), pltpu.VMEM((ec, tt), BF16),
                        pltpu.VMEM((d, tt), F32)],
        compiler_params=_params("arbitrary", "arbitrary"),
        name="peer_experts",
    )(h2.reshape(ntok, d), u, vt, a1, c1, b1, r2, x.reshape(ntok, d), mod)
    return out.reshape(bsz, s, d)


def _peer_ffn(x, mod, norm_g, wq, k1, k2, expert_u, expert_v):
    h2, a1, c1, b1, r2 = _peer_query(x, mod, norm_g, wq, k1, k2)
    return _peer_experts(h2, a1, c1, b1, r2, x, mod, expert_u, expert_v)


def kernel(x, c, l0_ada_w, l0_ada_b, l0_norm_mix, l0_w_in, l0_b_igate, l0_b_fgate, l0_conv_w, l0_mlstm_norm, l0_w_out, l0_norm_ffn, l0_peer_wq, l0_peer_k1, l0_peer_k2, l0_peer_u, l0_peer_v, l1_ada_w, l1_ada_b, l1_norm_mix, l1_w_qkv, l1_q_norm, l1_k_norm, l1_w_out, l1_norm_ffn, l1_peer_wq, l1_peer_k1, l1_peer_k2, l1_peer_u, l1_peer_v):
    mod0 = _ada(c, l0_ada_w, l0_ada_b)
    qkvo, hc, gates_col, gates_row = _l0_in(x, mod0, l0_norm_mix, l0_w_in, l0_b_igate, l0_b_fgate, l0_conv_w)
    hm = _mlstm(qkvo, gates_col, gates_row, l0_mlstm_norm)
    width = hm.shape[2]
    w_out0 = l0_w_out.astype(BF16)
    x = _projres([hm, hc], [w_out0[:width], w_out0[width:]], x, mod0, gate_row=2)
    x = _peer_ffn(x, mod0, l0_norm_ffn, l0_peer_wq, l0_peer_k1, l0_peer_k2, l0_peer_u, l0_peer_v)
    mod1 = _ada(c, l1_ada_w, l1_ada_b)
    q, k, v = _l1_qkv(x, mod1, l1_norm_mix, l1_w_qkv, l1_q_norm, l1_k_norm)
    o = _sb_attention(q, k, v)
    x = _projres([o], [l1_w_out.astype(BF16)], x, mod1, gate_row=2)
    x = _peer_ffn(x, mod1, l1_norm_ffn, l1_peer_wq, l1_peer_k1, l1_peer_k2, l1_peer_u, l1_peer_v)
    return x
```

```python
import functools
import math

import jax
import jax.numpy as jnp
from jax import lax
from jax.experimental import pallas as pl
from jax.experimental.pallas import tpu as pltpu

F32 = jnp.float32
BF16 = jnp.bfloat16
EPS = 1e-6
NT_DIMS = (((1,), (1,)), ((), ()))

LANES = 128
SUBLANES = 8
VMEM_LIMIT_BYTES = 56 * 1024 * 1024

MLSTM_HEADS = 4
MLSTM_CHUNK = 128
SB_HEADS = 16
PEER_HEADS = 8
PEER_TOPK = 16
ATT_LOG_CUTOFF = 100.0


def _params(*semantics):
    return pltpu.CompilerParams(dimension_semantics=semantics, vmem_limit_bytes=VMEM_LIMIT_BYTES)


def _modulate(x, g, shift, scale):
    ms = jnp.mean(x * x, axis=-1, keepdims=True)
    return x * lax.rsqrt(ms + EPS) * g * (1.0 + scale) + shift


def _log_sigmoid(x):
    return jnp.minimum(x, 0.0) - jnp.log1p(jnp.exp(-jnp.abs(x)))


def _ada_kernel(c_ref, w_ref, b_ref, o_ref):
    c = c_ref[...]
    o_ref[...] = jnp.dot(c * jax.nn.sigmoid(c), w_ref[...], preferred_element_type=F32,
                         precision=lax.Precision.HIGHEST) + b_ref[...]


def _ada(c, w, b):
    bsz, d = c.shape
    n = w.shape[1]
    tn = n // 4
    mod = pl.pallas_call(
        _ada_kernel,
        grid=(n // tn,),
        in_specs=[pl.BlockSpec((bsz, d), lambda j: (0, 0)),
                  pl.BlockSpec((d, tn), lambda j: (0, j)),
                  pl.BlockSpec((1, tn), lambda j: (0, j))],
        out_specs=pl.BlockSpec((bsz, tn), lambda j: (0, j)),
        out_shape=jax.ShapeDtypeStruct((bsz, n), F32),
        compiler_params=_params("arbitrary"),
        name="ada",
    )(c, w, b.reshape(1, n))
    return mod.reshape(bsz, 6, d)


def _l0_in_kernel(x_ref, mod_ref, g_ref, wa_ref, wc_ref, wg_ref, wgt_ref, bgc_ref, bgr_ref, cw_ref,
                  qkvo_ref, hc_ref, gc_ref, gr_ref, carry_ref, *, tm, width):
    @pl.when(pl.program_id(1) == 0)
    def _():
        carry_ref[...] = jnp.zeros_like(carry_ref)

    mod = mod_ref[0]
    h = _modulate(x_ref[0], g_ref[...], mod[0:1], mod[1:2]).astype(BF16)
    qkvo_ref[0] = jnp.dot(h, wa_ref[...], preferred_element_type=F32).astype(BF16)
    gc_ref[0] = jnp.dot(h, wg_ref[...], preferred_element_type=F32) + bgc_ref[...]
    gr_ref[0] = lax.dot_general(wgt_ref[...], h, NT_DIMS, preferred_element_type=F32) + bgr_ref[...]
    pc = jnp.dot(h, wc_ref[...], preferred_element_type=F32)
    cb, cc, ch = pc[:, :width], pc[:, width:2 * width], pc[:, 2 * width:]
    u = cc * ch
    prev = carry_ref[...]
    row = lax.broadcasted_iota(jnp.int32, u.shape, 0)
    u1 = jnp.where(row == 0, prev[SUBLANES - 1:SUBLANES], pltpu.roll(u, 1, 0))
    u2 = jnp.where(row == 0, prev[SUBLANES - 2:SUBLANES - 1],
                   jnp.where(row == 1, prev[SUBLANES - 1:SUBLANES], pltpu.roll(u, 2, 0)))
    cw = cw_ref[...]
    conv = cw[0:1] * u2 + cw[1:2] * u1 + cw[2:3] * u
    hc_ref[0] = (cb * conv).astype(BF16)
    carry_ref[...] = u[tm - SUBLANES:tm]


def _l0_in(x, mod, norm_g, w_in, b_igate, b_fgate, conv_w, tm=512):
    bsz, s, d = x.shape
    hn = MLSTM_HEADS
    width = conv_w.shape[1]
    wa = w_in[:, :4 * width].astype(BF16)
    wg_f = w_in[:, 4 * width:4 * width + 2 * hn]
    wc = w_in[:, 4 * width + 2 * hn:].astype(BF16)
    wg = jnp.pad(wg_f, ((0, 0), (0, LANES - 2 * hn))).astype(BF16)
    wgt = wg_f.T.astype(BF16)
    bias = jnp.concatenate([b_igate, b_fgate])
    bgc = jnp.pad(bias, (0, LANES - 2 * hn)).reshape(1, LANES)
    bgr = bias.reshape(2 * hn, 1)
    full = lambda *shape: pl.BlockSpec(shape, lambda b, i: (0,) * len(shape))
    return pl.pallas_call(
        functools.partial(_l0_in_kernel, tm=tm, width=width),
        grid=(bsz, s // tm),
        in_specs=[pl.BlockSpec((1, tm, d), lambda b, i: (b, i, 0)),
                  pl.BlockSpec((1, 6, d), lambda b, i: (b, 0, 0)),
                  full(1, d), full(d, 4 * width), full(d, 3 * width), full(d, LANES), full(2 * hn, d),
                  full(1, LANES), full(2 * hn, 1), full(3, width)],
        out_specs=[pl.BlockSpec((1, tm, 4 * width), lambda b, i: (b, i, 0)),
                   pl.BlockSpec((1, tm, width), lambda b, i: (b, i, 0)),
                   pl.BlockSpec((1, tm, LANES), lambda b, i: (b, i, 0)),
                   pl.BlockSpec((1, 2 * hn, tm), lambda b, i: (b, 0, i))],
        out_shape=[jax.ShapeDtypeStruct((bsz, s, 4 * width), BF16),
                   jax.ShapeDtypeStruct((bsz, s, width), BF16),
                   jax.ShapeDtypeStruct((bsz, s, LANES), F32),
                   jax.ShapeDtypeStruct((bsz, 2 * hn, s), F32)],
        scratch_shapes=[pltpu.VMEM((SUBLANES, width), F32)],
        compiler_params=_params("arbitrary", "arbitrary"),
        name="l0_in",
    )(x, mod, norm_g.reshape(1, d), wa, wc, wg, wgt, bgc, bgr, conv_w)


def _mlstm_kernel(qkvo_ref, gc_ref, gr_ref, nrm_ref, hm_ref, s_ref, n_ref, m_ref, *, nb, hn, dh, chunk, nchunk):
    @pl.when(pl.program_id(1) == 0)
    def _():
        s_ref[...] = jnp.zeros_like(s_ref)
        n_ref[...] = jnp.zeros_like(n_ref)
        m_ref[...] = jnp.zeros_like(m_ref)

    width = hn * dh
    scale = dh ** -0.5
    ri = lax.broadcasted_iota(jnp.int32, (chunk, chunk), 0)
    ci = lax.broadcasted_iota(jnp.int32, (chunk, chunk), 1)
    causal = ci <= ri
    for c, bb, h in [(c, bb, h) for c in range(nchunk) for bb in range(nb) for h in range(hn)]:
        r0 = c * chunk
        gcol = gc_ref[bb, r0:r0 + chunk, :]
        grow = gr_ref[bb, :, r0:r0 + chunk]
        st = bb * hn + h
        q = qkvo_ref[bb, r0:r0 + chunk, h * dh:(h + 1) * dh]
        k = qkvo_ref[bb, r0:r0 + chunk, width + h * dh:width + (h + 1) * dh]
        v = qkvo_ref[bb, r0:r0 + chunk, 2 * width + h * dh:2 * width + (h + 1) * dh]
        o = qkvo_ref[bb, r0:r0 + chunk, 3 * width + h * dh:3 * width + (h + 1) * dh]
        li_col = gcol[:, h:h + 1]
        lf_col = _log_sigmoid(gcol[:, hn + h:hn + h + 1])
        li_row = grow[h:h + 1, :]
        lf_row = _log_sigmoid(grow[hn + h:hn + h + 1, :])
        b_col = jnp.sum(jnp.where(causal, lf_row, 0.0), axis=1, keepdims=True)
        b_row = jnp.sum(jnp.where(ri <= ci, lf_col, 0.0), axis=0, keepdims=True)
        g = b_col[chunk - 1:chunk, :]
        a_col = g - b_col + li_col
        s_prev = s_ref[st]
        n_prev = n_ref[st]
        m_prev = m_ref[st]
        log_d = jnp.where(causal, b_col - b_row + li_row, -jnp.inf)
        inter_log = b_col + m_prev
        m_t = jnp.maximum(inter_log, jnp.max(log_d, axis=1, keepdims=True))
        qk = lax.dot_general(q, k, NT_DIMS, preferred_element_type=F32) * scale * jnp.exp(log_d - m_t)
        inter_w = jnp.exp(inter_log - m_t) * scale
        num = (jnp.dot(qk.astype(BF16), v, preferred_element_type=F32)
               + inter_w * jnp.dot(q, s_prev.astype(BF16), preferred_element_type=F32))
        den = (jnp.sum(qk, axis=1, keepdims=True)
               + inter_w * jnp.sum(q.astype(F32) * n_prev, axis=1, keepdims=True))
        hv = num / jnp.maximum(jnp.abs(den), jnp.exp(-m_t))
        hv = hv * lax.rsqrt(jnp.mean(hv * hv, axis=-1, keepdims=True) + EPS) * nrm_ref[:, h * dh:(h + 1) * dh]
        hm_ref[bb, r0:r0 + chunk, h * dh:(h + 1) * dh] = (hv * jax.nn.sigmoid(o.astype(F32))).astype(BF16)
        m_new = jnp.maximum(g + m_prev, jnp.max(a_col, axis=0, keepdims=True))
        decay = jnp.exp(g + m_prev - m_new)
        kw = k.astype(F32) * jnp.exp(a_col - m_new)
        s_ref[st] = decay * s_prev + jnp.dot(kw.T.astype(BF16), v, preferred_element_type=F32)
        n_ref[st] = decay * n_prev + jnp.sum(kw, axis=0, keepdims=True)
        m_ref[st] = m_new


def _mlstm(qkvo, gates_col, gates_row, mlstm_norm, tq=512, nb=1):
    bsz, s, w4 = qkvo.shape
    hn = MLSTM_HEADS
    width = w4 // 4
    dh = width // hn
    nb = math.gcd(nb, bsz)
    return pl.pallas_call(
        functools.partial(_mlstm_kernel, nb=nb, hn=hn, dh=dh, chunk=MLSTM_CHUNK, nchunk=tq // MLSTM_CHUNK),
        grid=(bsz // nb, s // tq),
        in_specs=[pl.BlockSpec((nb, tq, w4), lambda b, i: (b, i, 0)),
                  pl.BlockSpec((nb, tq, LANES), lambda b, i: (b, i, 0)),
                  pl.BlockSpec((nb, 2 * hn, tq), lambda b, i: (b, 0, i)),
                  pl.BlockSpec((1, width), lambda b, i: (0, 0))],
        out_specs=pl.BlockSpec((nb, tq, width), lambda b, i: (b, i, 0)),
        out_shape=jax.ShapeDtypeStruct((bsz, s, width), BF16),
        scratch_shapes=[pltpu.VMEM((nb * hn, dh, dh), F32), pltpu.VMEM((nb * hn, 1, dh), F32),
                        pltpu.VMEM((nb * hn, 1, 1), F32)],
        compiler_params=_params("arbitrary", "arbitrary"),
        name="mlstm",
    )(qkvo, gates_col, gates_row, mlstm_norm.reshape(1, width))


def _projres_kernel(*refs, n_act, gate_row):
    acts, ws = refs[:n_act], refs[n_act:2 * n_act]
    x_ref, mod_ref, o_ref = refs[2 * n_act:]
    y = jnp.dot(acts[0][0], ws[0][...], preferred_element_type=F32)
    for a_ref, w_ref in zip(acts[1:], ws[1:]):
        y = y + jnp.dot(a_ref[0], w_ref[...], preferred_element_type=F32)
    gate = mod_ref[0][gate_row:gate_row + 1]
    o_ref[0] = x_ref[0] + (1.0 + gate) * y


def _projres(acts, ws, x, mod, gate_row, tm=512):
    bsz, s, d = x.shape
    n_act = len(acts)
    in_specs = ([pl.BlockSpec((1, tm, a.shape[2]), lambda b, i: (b, i, 0)) for a in acts]
                + [pl.BlockSpec(w.shape, lambda b, i: (0, 0)) for w in ws]
                + [pl.BlockSpec((1, tm, d), lambda b, i: (b, i, 0)),
                   pl.BlockSpec((1, 6, d), lambda b, i: (b, 0, 0))])
    return pl.pallas_call(
        functools.partial(_projres_kernel, n_act=n_act, gate_row=gate_row),
        grid=(bsz, s // tm),
        in_specs=in_specs,
        out_specs=pl.BlockSpec((1, tm, d), lambda b, i: (b, i, 0)),
        out_shape=jax.ShapeDtypeStruct((bsz, s, d), F32),
        compiler_params=_params("arbitrary", "arbitrary"),
        name="projres",
    )(*acts, *ws, x, mod)


def _l1_qkv_kernel(x_ref, mod_ref, g_ref, w_ref, qn_ref, kn_ref, q_ref, k_ref, v_ref, *, d, dh):
    mod = mod_ref[0]
    h = _modulate(x_ref[0], g_ref[...], mod[0:1], mod[1:2]).astype(BF16)
    y = jnp.dot(h, w_ref[...], preferred_element_type=F32)
    tm = y.shape[0]
    low = lax.broadcasted_iota(jnp.int32, (tm, LANES), 1) < dh
    scale = dh ** -0.5

    def headnorm(t, gain):
        sq = t * t
        s_low = jnp.sum(jnp.where(low, sq, 0.0), axis=-1, keepdims=True)
        s_all = jnp.sum(sq, axis=-1, keepdims=True)
        ms = jnp.where(low, s_low, s_all - s_low) * (1.0 / dh)
        return t * lax.rsqrt(ms + EPS) * gain

    for p in range(d // LANES):
        sl = slice(p * LANES, (p + 1) * LANES)
        q_ref[0, :, sl] = (headnorm(y[:, sl], qn_ref[...]) * scale).astype(BF16)
        k_ref[0, :, sl] = headnorm(y[:, d + p * LANES:d + (p + 1) * LANES], kn_ref[...]).astype(BF16)
    v_ref[0] = y[:, 2 * d:].astype(BF16)


def _l1_qkv(x, mod, norm_g, w_qkv, q_norm, k_norm, tm=512):
    bsz, s, d = x.shape
    dh = q_norm.shape[0]
    assert LANES == 2 * dh
    full = lambda *shape: pl.BlockSpec(shape, lambda b, i: (0,) * len(shape))
    tok = pl.BlockSpec((1, tm, d), lambda b, i: (b, i, 0))
    return pl.pallas_call(
        functools.partial(_l1_qkv_kernel, d=d, dh=dh),
        grid=(bsz, s // tm),
        in_specs=[tok, pl.BlockSpec((1, 6, d), lambda b, i: (b, 0, 0)), full(1, d), full(d, 3 * d),
                  full(1, LANES), full(1, LANES)],
        out_specs=[tok, tok, tok],
        out_shape=[jax.ShapeDtypeStruct((bsz, s, d), BF16)] * 3,
        compiler_params=_params("arbitrary", "arbitrary"),
        name="l1_qkv",
    )(x, mod, norm_g.reshape(1, d), w_qkv.astype(BF16), jnp.tile(q_norm, 2).reshape(1, LANES),
      jnp.tile(k_norm, 2).reshape(1, LANES))


def _sb_kernel(q_ref, k_ref, v_ref, o_ref, carry_ref, acc_ref, *, tq, dh):
    qi = pl.program_id(2)
    q = q_ref[0]
    lane = lax.broadcasted_iota(jnp.int32, (tq, LANES), 1)
    ri = lax.broadcasted_iota(jnp.int32, (tq, tq), 0)
    ci = lax.broadcasted_iota(jnp.int32, (tq, tq), 1)
    strict = ci < ri
    rhs = jnp.concatenate([jnp.where(ri > ci, 1.0, 0.0), jnp.ones((tq, LANES), F32)], axis=1).astype(BF16)
    rep = tq // LANES
    n_heads = LANES // dh
    hmasks = [(lane >= hh * dh) & (lane < (hh + 1) * dh) for hh in range(n_heads)]
    qms = [jnp.where(m, q, jnp.zeros_like(q)) for m in hmasks]

    def load(kb):
        start = pl.multiple_of(kb * tq, tq)
        return k_ref[0, pl.ds(start, tq), :], v_ref[0, pl.ds(start, tq), :]

    def head_block(hh, kblk, vblk, diag, carry):
        z = lax.dot_general(qms[hh], kblk, NT_DIMS, preferred_element_type=F32)
        lsn = _log_sigmoid(-z)
        if diag:
            lsn = jnp.where(strict, lsn, 0.0)
        hi = lsn.astype(BF16)
        lo = (lsn - hi.astype(F32)).astype(BF16)
        rr = jnp.dot(hi, rhs, preferred_element_type=F32) + jnp.dot(lo, rhs, preferred_element_type=F32)
        logit = z + lsn + rr[:, :tq]
        if carry is not None:
            logit = logit + jnp.concatenate([carry] * rep, axis=1)
        p = jnp.exp(logit)
        if diag:
            p = jnp.where(strict, p, 0.0)
        return jnp.dot(p.astype(BF16), vblk, preferred_element_type=F32), rr[:, tq:]

    k0, v0 = load(qi)

    @pl.when(qi == 0)
    def _():
        for hh in range(n_heads):
            acc_ref[hh], carry_ref[hh] = head_block(hh, k0, v0, True, None)

    @pl.when(qi > 0)
    def _():
        k1, v1 = load(qi - 1)
        for hh in range(n_heads):
            pv0, rows0 = head_block(hh, k0, v0, True, None)
            pv1, rows1 = head_block(hh, k1, v1, False, rows0)
            acc_ref[hh] = pv0 + pv1
            carry_ref[hh] = rows0 + rows1

    def worst_survival():
        worst = jnp.max(carry_ref[0])
        for hh in range(1, n_heads):
            worst = jnp.maximum(worst, jnp.max(carry_ref[hh]))
        return worst

    def cond(state):
        it, worst = state
        return (it < qi) & (worst > -ATT_LOG_CUTOFF)

    def body(state):
        it, _ = state
        kblk, vblk = load(qi - 1 - it)
        for hh in range(n_heads):
            pv, rows = head_block(hh, kblk, vblk, False, carry_ref[hh])
            acc_ref[hh] += pv
            carry_ref[hh] += rows
        return it + 1, worst_survival()

    lax.while_loop(cond, body, (jnp.int32(1), worst_survival()))
    out = acc_ref[0]
    for hh in range(1, n_heads):
        out = jnp.where(hmasks[hh], acc_ref[hh], out)
    o_ref[0] = out.astype(BF16)


def _sb_attention(q, k, v, tq=256):
    bsz, s, d = q.shape
    dh = d // SB_HEADS
    npair = d // LANES
    return pl.pallas_call(
        functools.partial(_sb_kernel, tq=tq, dh=dh),
        grid=(bsz, npair, s // tq),
        in_specs=[pl.BlockSpec((1, tq, LANES), lambda b, p, i: (b, i, p)),
                  pl.BlockSpec((1, s, LANES), lambda b, p, i: (b, 0, p)),
                  pl.BlockSpec((1, s, LANES), lambda b, p, i: (b, 0, p))],
        out_specs=pl.BlockSpec((1, tq, LANES), lambda b, p, i: (b, i, p)),
        out_shape=jax.ShapeDtypeStruct((bsz, s, d), BF16),
        scratch_shapes=[pltpu.VMEM((LANES // dh, tq, LANES), F32), pltpu.VMEM((LANES // dh, tq, LANES), F32)],
        compiler_params=_params("arbitrary", "arbitrary", "arbitrary"),
        name="sb_attn",
    )(q, k, v)


def _sorter_pairs(lo, hi):
    def merge(lo, hi, r):
        step = r * 2
        if step < hi - lo:
            yield from merge(lo, hi, step)
            yield from merge(lo + r, hi, step)
            yield from ((i, i + r) for i in range(lo + r, hi - r, step))
        else:
            yield (lo, lo + r)

    if hi - lo >= 1:
        mid = lo + (hi - lo) // 2
        yield from _sorter_pairs(lo, mid)
        yield from _sorter_pairs(mid + 1, hi)
        yield from merge(lo, hi, 1)


def _exchange(x, i, j):
    x[i], x[j] = jnp.maximum(x[i], x[j]), jnp.minimum(x[i], x[j])


def _sort_bitonic(x):
    d = len(x) // 2
    while d:
        for i in range(len(x)):
            if not i & d:
                _exchange(x, i, i + d)
        d //= 2


def _top_values(s_ref, vals_ref):
    n, width = s_ref.shape
    n_tile = n // SUBLANES
    ok_total = 0.0
    for c0 in range(0, width, LANES):
        cols = slice(c0, c0 + LANES)
        rows = [s_ref[t * SUBLANES:(t + 1) * SUBLANES, cols] for t in range(n_tile)]
        x = list(rows)
        for i, j in _sorter_pairs(0, n_tile - 1):
            _exchange(x, i, j)
        for shift in (SUBLANES // 2, SUBLANES // 4, SUBLANES // 8):
            other = [pltpu.roll(v, shift, 0) for v in x]
            if len(x) < PEER_TOPK:
                x = x + other[::-1]
            else:
                x = [jnp.maximum(a, b) for a, b in zip(x, other[::-1])]
            _sort_bitonic(x)
        for i in range(PEER_TOPK):
            vals_ref[i:i + 1, cols] = x[i][0:1]
        gap = x[0] - x[1]
        for i in range(1, PEER_TOPK - 1):
            gap = jnp.minimum(gap, x[i] - x[i + 1])
        reach = jnp.zeros((SUBLANES, LANES), F32)
        for r in rows:
            reach = reach + jnp.where(r >= x[PEER_TOPK - 1], 1.0, 0.0)
        reach = jnp.sum(reach, axis=0, keepdims=True)
        ok = jnp.where((gap[0:1] > 0.0) & (reach == float(PEER_TOPK)), 1.0, 0.0)
        ok_total = ok_total + jnp.sum(ok)
    return ok_total == float(width)


def _rank_top(s_ref, rank_ref, vals_ref):
    n, width = s_ref.shape
    rowf = lax.broadcasted_iota(jnp.int32, (n, LANES), 0).astype(F32)
    slot = lax.broadcasted_iota(jnp.int32, (PEER_TOPK, LANES), 0)

    def body(i, carry):
        cur, rank, vals = carry
        mx = jnp.max(cur, axis=0, keepdims=True)
        hit = rowf == jnp.min(jnp.where(cur == mx, rowf, float(n)), axis=0, keepdims=True)
        return (jnp.where(hit, -jnp.inf, cur), jnp.where(hit, jnp.asarray(i, F32), rank),
                jnp.where(slot == i, mx, vals))

    for c0 in range(0, width, LANES):
        cols = slice(c0, c0 + LANES)
        init = (s_ref[:, cols], jnp.full((n, LANES), float(PEER_TOPK), F32), jnp.zeros((PEER_TOPK, LANES), F32))
        _, rank, vals = lax.fori_loop(0, PEER_TOPK, body, init)
        rank_ref[:, cols] = rank
        vals_ref[:, cols] = vals


def _twice_bf16(x):
    bits = pltpu.bitcast(x.astype(BF16).astype(F32), jnp.uint32)
    return bits | (bits >> 16)


def _peer_query_kernel(x_ref, mod_ref, g_ref, wq_ref, k1_ref, k2_ref,
                       h_ref, a1_ref, c1_ref, b1_ref, r2_ref,
                       q_ref, s_ref, r_ref, v_ref, cand_ref, crank_ref, cval_ref, *, heads, nk):
    topk = PEER_TOPK
    mod = mod_ref[0]
    h = _modulate(x_ref[0], g_ref[...], mod[3:4], mod[4:5]).astype(BF16)
    h_ref[0] = h
    q_ref[...] = jnp.dot(h, wq_ref[...], preferred_element_type=F32).astype(BF16)
    tp = q_ref.shape[0]
    pairs = [(i, topk // (i + 1)) for i in range(topk)]
    ncand = sum(n for _, n in pairs)
    ncand_pad = cand_ref.shape[0]

    for hd in range(heads):
        for half, keys_ref in enumerate((k1_ref, k2_ref)):
            qh = q_ref[:, (2 * hd + half) * nk:(2 * hd + half + 1) * nk]
            s_ref[2 * hd + half] = lax.dot_general(keys_ref[...], qh, NT_DIMS, preferred_element_type=F32)

    def build_candidates():
        off = 0
        for i, n in pairs:
            cand_ref[off:off + n, :] = v_ref[0, i:i + 1, :] + v_ref[1, 0:n, :]
            off += n
        cand_ref[ncand:ncand_pad, :] = jnp.full((ncand_pad - ncand, tp), -jnp.inf, F32)

    def emit(hd, a1, cnt1, b1, r2):
        a1_ref[hd] = _twice_bf16(a1)
        c1_ref[hd] = _twice_bf16(cnt1)
        b1w = pltpu.bitcast(b1.astype(BF16), jnp.uint32)
        r2w = pltpu.bitcast(r2.astype(BF16), jnp.uint32)
        for lt in range(tp // LANES):
            b1_ref[lt, hd] = b1w[:, lt * LANES:(lt + 1) * LANES]
            r2_ref[lt, hd] = r2w[:, lt * LANES:(lt + 1) * LANES]

    def softmax_norm(picked):
        cand = cand_ref[...]
        return jnp.sum(picked * jnp.exp(cand - cand[0:1]), axis=0, keepdims=True)

    def pair_counts(picked):
        counts, off = [], 0
        for _, n in pairs:
            counts.append(jnp.sum(picked[off:off + n], axis=0, keepdims=True))
            off += n
        return counts

    for hd in range(heads):
        s1_ref, s2_ref = s_ref.at[2 * hd], s_ref.at[2 * hd + 1]
        distinct = _top_values(s1_ref, v_ref.at[0]) & _top_values(s2_ref, v_ref.at[1])
        build_candidates()
        distinct = distinct & _top_values(cand_ref, cval_ref)
        picked = jnp.where(cand_ref[...] >= cval_ref[topk - 1:topk, :], 1.0, 0.0)
        half_over_z = 0.5 / softmax_norm(picked)
        counts = pair_counts(picked) + [jnp.zeros((1, tp), F32)]
        for lt in range(tp // LANES):
            cols = slice(lt * LANES, (lt + 1) * LANES)
            s1 = s1_ref[:, cols]
            cnt1 = jnp.broadcast_to(counts[0][:, cols], (nk, LANES))
            for i in range(topk):
                cnt1 = jnp.where(s1 < v_ref[0, i:i + 1, cols], counts[i + 1][:, cols], cnt1)
            a1 = jnp.where(s1 >= v_ref[0, topk - 1:topk, cols], jnp.exp(s1 - v_ref[0, 0:1, cols]), 0.0) * half_over_z[:, cols]
            a1_ref[hd, :, cols] = _twice_bf16(a1)
            c1_ref[hd, :, cols] = _twice_bf16(cnt1)
            s2 = s2_ref[:, cols]
            r2 = jnp.zeros((nk, LANES), F32)
            for i in range(topk):
                r2 = jnp.where(s2 < v_ref[1, i:i + 1, cols], float(i + 1), r2)
            b1 = jnp.where(s2 >= v_ref[1, topk - 1:topk, cols], jnp.exp(s2 - v_ref[1, 0:1, cols]), 0.0)
            b1_ref[lt, hd] = pltpu.bitcast(b1.astype(BF16), jnp.uint32)
            r2_ref[lt, hd] = pltpu.bitcast(r2.astype(BF16), jnp.uint32)

        @pl.when(jnp.logical_not(distinct))
        def _():
            _rank_top(s1_ref, r_ref.at[0], v_ref.at[0])
            _rank_top(s2_ref, r_ref.at[1], v_ref.at[1])
            build_candidates()
            _rank_top(cand_ref, crank_ref, cval_ref)
            picked = jnp.where(crank_ref[...] < topk, 1.0, 0.0)
            z = softmax_norm(picked)
            r1, r2 = r_ref[0], r_ref[1]
            cnt1 = jnp.zeros((nk, tp), F32)
            for i, c_i in enumerate(pair_counts(picked)):
                cnt1 = jnp.where(r1 == float(i), c_i, cnt1)
            a1 = jnp.where(r1 < topk, jnp.exp(s1_ref[...] - v_ref[0, 0:1, :]), 0.0) * (0.5 / z)
            b1 = jnp.where(r2 < topk, jnp.exp(s2_ref[...] - v_ref[1, 0:1, :]), 0.0)
            emit(hd, a1, cnt1, b1, r2)


def _peer_query(x, mod, norm_g, wq, k1, k2, tp=256):
    bsz, s, d = x.shape
    nk = k1.shape[0]
    heads = wq.shape[1] // (2 * k1.shape[1])
    assert heads == PEER_HEADS and k1.shape[1] == nk
    ntok = bsz * s
    nt = s // tp
    full = lambda *shape: pl.BlockSpec(shape, lambda b, i: (0,) * len(shape))
    gate_spec = pl.BlockSpec((heads, nk, tp), lambda b, i: (0, 0, b * nt + i))
    gate_shape = jax.ShapeDtypeStruct((heads, nk, ntok), jnp.uint32)
    pair_spec = pl.BlockSpec((tp // LANES, heads, nk // 2, LANES), lambda b, i: (b * nt + i, 0, 0, 0))
    pair_shape = jax.ShapeDtypeStruct((ntok // LANES, heads, nk // 2, LANES), jnp.uint32)
    ncand_pad = 64
    return pl.pallas_call(
        functools.partial(_peer_query_kernel, heads=heads, nk=nk),
        grid=(bsz, nt),
        in_specs=[pl.BlockSpec((1, tp, d), lambda b, i: (b, i, 0)),
                  pl.BlockSpec((1, 6, d), lambda b, i: (b, 0, 0)),
                  full(1, d), full(d, wq.shape[1]), full(nk, nk), full(nk, nk)],
        out_specs=[pl.BlockSpec((1, tp, d), lambda b, i: (b, i, 0)), gate_spec, gate_spec, pair_spec, pair_spec],
        out_shape=[jax.ShapeDtypeStruct((bsz, s, d), BF16), gate_shape, gate_shape, pair_shape, pair_shape],
        scratch_shapes=[pltpu.VMEM((tp, wq.shape[1]), BF16), pltpu.VMEM((2 * heads, nk, tp), F32),
                        pltpu.VMEM((2, nk, tp), F32), pltpu.VMEM((2, PEER_TOPK, tp), F32),
                        pltpu.VMEM((ncand_pad, tp), F32), pltpu.VMEM((ncand_pad, tp), F32),
                        pltpu.VMEM((PEER_TOPK, tp), F32)],
        compiler_params=_params("arbitrary", "arbitrary"),
        name="peer_query",
    )(x, mod, norm_g.reshape(1, d), wq.astype(BF16), k1.astype(BF16), k2.astype(BF16))


def _peer_expert_kernel(h_ref, u_ref, vt_ref, a1_ref, c1_ref, b1_ref, r2_ref, x_ref, mod_ref, o_ref,
                        s_ref, w_ref, acc_ref, *, heads, nk, n_e1):
    j = pl.program_id(1)

    @pl.when(j == 0)
    def _():
        acc_ref[...] = jnp.zeros_like(acc_ref)

    tt = h_ref.shape[0]
    zero = jnp.zeros((nk, LANES), BF16)

    def bcast_rows(ref, hd, e, cols):
        return pltpu.bitcast(jnp.broadcast_to(ref[hd, e:e + 1, cols], (nk // 2, LANES)), BF16)

    s = lax.dot_general(u_ref[...], h_ref[...], NT_DIMS, preferred_element_type=F32)
    for lt in range(tt // LANES):
        s_ref[lt] = s[:, lt * LANES:(lt + 1) * LANES]

    sw = w_ref.shape[2]
    n_stage = tt // sw

    def weights(stage):
        for t in range(sw // LANES):
            lt = stage * (sw // LANES) + t
            cols = pl.ds(pl.multiple_of(lt * LANES, LANES), LANES)
            for e in range(n_e1):
                rows = slice(e * nk, (e + 1) * nk)
                s = s_ref[lt, rows, :].astype(BF16)
                act = s * (1.0 + lax.erf(s * (1.0 / math.sqrt(2.0))))
                gate = zero
                for hd in range(heads):
                    cnt = bcast_rows(c1_ref, hd, e, cols)
                    a1 = bcast_rows(a1_ref, hd, e, cols)
                    r2 = pltpu.bitcast(r2_ref[lt, hd], BF16)
                    b1 = pltpu.bitcast(b1_ref[lt, hd], BF16)
                    gate = gate + jnp.where(r2 < cnt, b1, zero) * a1
                w_ref[stage % 2, rows, t * LANES:(t + 1) * LANES] = act * gate

    def values(stage):
        cols = pl.ds(pl.multiple_of(stage * sw, sw), sw)
        acc_ref[:, cols] += jnp.dot(vt_ref[...], w_ref[stage % 2], preferred_element_type=F32)

    weights(0)

    def body(k, _):
        weights(k)
        values(k - 1)
        return 0

    lax.fori_loop(1, n_stage, body, 0)
    values(n_stage - 1)

    @pl.when(j == pl.num_programs(1) - 1)
    def _():
        o_ref[...] = x_ref[...] + (1.0 + mod_ref[0][5:6]) * acc_ref[...].T


def _peer_experts(h2, a1, c1, b1, r2, x, mod, expert_u, expert_v, tt=1024, ec=1024):
    bsz, s, d = x.shape
    ntok = bsz * s
    heads, nk, _ = a1.shape
    n_exp = expert_u.shape[0]
    n_e1 = ec // nk
    tt = min(tt, s)
    tiles_per_seq = s // tt
    u = expert_u.astype(BF16)
    vt = expert_v.T.astype(BF16)
    out = pl.pallas_call(
        functools.partial(_peer_expert_kernel, heads=heads, nk=nk, n_e1=n_e1),
        grid=(ntok // tt, n_exp // ec),
        in_specs=[pl.BlockSpec((tt, d), lambda i, j: (i, 0)),
                  pl.BlockSpec((ec, d), lambda i, j: (j, 0)),
                  pl.BlockSpec((d, ec), lambda i, j: (0, j)),
                  pl.BlockSpec((heads, n_e1, tt), lambda i, j: (0, j, i)),
                  pl.BlockSpec((heads, n_e1, tt), lambda i, j: (0, j, i)),
                  pl.BlockSpec((tt // LANES, heads, nk // 2, LANES), lambda i, j: (i, 0, 0, 0)),
                  pl.BlockSpec((tt // LANES, heads, nk // 2, LANES), lambda i, j: (i, 0, 0, 0)),
                  pl.BlockSpec((tt, d), lambda i, j: (i, 0)),
                  pl.BlockSpec((1, 6, d), lambda i, j: (i // tiles_per_seq, 0, 0))],
        out_specs=pl.BlockSpec((tt, d), lambda i, j: (i, 0)),
        out_shape=jax.ShapeDtypeStruct((ntok, d), F32),
        scratch_shapes=[pltpu.VMEM((tt // LANES, ec, LANES), F32), pltpu.VMEM((2, ec, 2 * LANES), BF16),
                        pltpu.VMEM((d, tt), F32)],
        compiler_params=_params("arbitrary", "arbitrary"),
        name="peer_experts",
    )(h2.reshape(ntok, d), u, vt, a1, c1, b1, r2, x.reshape(ntok, d), mod)
    return out.reshape(bsz, s, d)


def _peer_ffn(x, mod, norm_g, wq, k1, k2, expert_u, expert_v):
    h2, a1, c1, b1, r2 = _peer_query(x, mod, norm_g, wq, k1, k2)
    return _peer_experts(h2, a1, c1, b1, r2, x, mod, expert_u, expert_v)


def kernel(x, c, l0_ada_w, l0_ada_b, l0_norm_mix, l0_w_in, l0_b_igate, l0_b_fgate, l0_conv_w, l0_mlstm_norm, l0_w_out, l0_norm_ffn, l0_peer_wq, l0_peer_k1, l0_peer_k2, l0_peer_u, l0_peer_v, l1_ada_w, l1_ada_b, l1_norm_mix, l1_w_qkv, l1_q_norm, l1_k_norm, l1_w_out, l1_norm_ffn, l1_peer_wq, l1_peer_k1, l1_peer_k2, l1_peer_u, l1_peer_v):
    mod0 = _ada(c, l0_ada_w, l0_ada_b)
    qkvo, hc, gates_col, gates_row = _l0_in(x, mod0, l0_norm_mix, l0_w_in, l0_b_igate, l0_b_fgate, l0_conv_w)
    hm = _mlstm(qkvo, gates_col, gates_row, l0_mlstm_norm)
    width = hm.shape[2]
    w_out0 = l0_w_out.astype(BF16)
    x = _projres([hm, hc], [w_out0[:width], w_out0[width:]], x, mod0, gate_row=2)
    x = _peer_ffn(x, mod0, l0_norm_ffn, l0_peer_wq, l0_peer_k1, l0_peer_k2, l0_peer_u, l0_peer_v)
    mod1 = _ada(c, l1_ada_w, l1_ada_b)
    q, k, v = _l1_qkv(x, mod1, l1_norm_mix, l1_w_qkv, l1_q_norm, l1_k_norm)
    o = _sb_attention(q, k, v)
    x = _projres([o], [l1_w_out.astype(BF16)], x, mod1, gate_row=2)
    x = _peer_ffn(x, mod1, l1_norm_ffn, l1_peer_wq, l1_peer_k1, l1_peer_k2, l1_peer_u, l1_peer_v)
    return x
```

```python
import functools
import math

import jax
import jax.numpy as jnp
from jax import lax
from jax.experimental import pallas as pl
from jax.experimental.pallas import tpu as pltpu

F32 = jnp.float32
BF16 = jnp.bfloat16
EPS = 1e-6
NT_DIMS = (((1,), (1,)), ((), ()))

LANES = 128
SUBLANES = 8
VMEM_LIMIT_BYTES = 56 * 1024 * 1024

MLSTM_HEADS = 4
MLSTM_CHUNK = 128
SB_HEADS = 16
PEER_HEADS = 8
PEER_TOPK = 16
ATT_LOG_CUTOFF = 100.0


def _params(*semantics):
    return pltpu.CompilerParams(dimension_semantics=semantics, vmem_limit_bytes=VMEM_LIMIT_BYTES)


def _modulate(x, g, shift, scale):
    ms = jnp.mean(x * x, axis=-1, keepdims=True)
    return x * lax.rsqrt(ms + EPS) * g * (1.0 + scale) + shift


def _log_sigmoid(x):
    return jnp.minimum(x, 0.0) - jnp.log1p(jnp.exp(-jnp.abs(x)))


def _ada_kernel(c_ref, w_ref, b_ref, o_ref):
    c = c_ref[...]
    o_ref[...] = jnp.dot(c * jax.nn.sigmoid(c), w_ref[...], preferred_element_type=F32,
                         precision=lax.Precision.HIGHEST) + b_ref[...]


def _ada(c, w, b):
    bsz, d = c.shape
    n = w.shape[1]
    tn = n // 4
    mod = pl.pallas_call(
        _ada_kernel,
        grid=(n // tn,),
        in_specs=[pl.BlockSpec((bsz, d), lambda j: (0, 0)),
                  pl.BlockSpec((d, tn), lambda j: (0, j)),
                  pl.BlockSpec((1, tn), lambda j: (0, j))],
        out_specs=pl.BlockSpec((bsz, tn), lambda j: (0, j)),
        out_shape=jax.ShapeDtypeStruct((bsz, n), F32),
        compiler_params=_params("arbitrary"),
        name="ada",
    )(c, w, b.reshape(1, n))
    return mod.reshape(bsz, 6, d)


def _l0_in_kernel(x_ref, mod_ref, g_ref, wa_ref, wc_ref, wg_ref, wgt_ref, bgc_ref, bgr_ref, cw_ref,
                  qkvo_ref, hc_ref, gc_ref, gr_ref, carry_ref, *, tm, width):
    @pl.when(pl.program_id(1) == 0)
    def _():
        carry_ref[...] = jnp.zeros_like(carry_ref)

    mod = mod_ref[0]
    h = _modulate(x_ref[0], g_ref[...], mod[0:1], mod[1:2]).astype(BF16)
    qkvo_ref[0] = jnp.dot(h, wa_ref[...], preferred_element_type=F32).astype(BF16)
    gc_ref[0] = jnp.dot(h, wg_ref[...], preferred_element_type=F32) + bgc_ref[...]
    gr_ref[0] = lax.dot_general(wgt_ref[...], h, NT_DIMS, preferred_element_type=F32) + bgr_ref[...]
    pc = jnp.dot(h, wc_ref[...], preferred_element_type=F32)
    cb, cc, ch = pc[:, :width], pc[:, width:2 * width], pc[:, 2 * width:]
    u = cc * ch
    prev = carry_ref[...]
    row = lax.broadcasted_iota(jnp.int32, u.shape, 0)
    u1 = jnp.where(row == 0, prev[SUBLANES - 1:SUBLANES], pltpu.roll(u, 1, 0))
    u2 = jnp.where(row == 0, prev[SUBLANES - 2:SUBLANES - 1],
                   jnp.where(row == 1, prev[SUBLANES - 1:SUBLANES], pltpu.roll(u, 2, 0)))
    cw = cw_ref[...]
    conv = cw[0:1] * u2 + cw[1:2] * u1 + cw[2:3] * u
    hc_ref[0] = (cb * conv).astype(BF16)
    carry_ref[...] = u[tm - SUBLANES:tm]


def _l0_in(x, mod, norm_g, w_in, b_igate, b_fgate, conv_w, tm=512):
    bsz, s, d = x.shape
    hn = MLSTM_HEADS
    width = conv_w.shape[1]
    wa = w_in[:, :4 * width].astype(BF16)
    wg_f = w_in[:, 4 * width:4 * width + 2 * hn]
    wc = w_in[:, 4 * width + 2 * hn:].astype(BF16)
    wg = jnp.pad(wg_f, ((0, 0), (0, LANES - 2 * hn))).astype(BF16)
    wgt = wg_f.T.astype(BF16)
    bias = jnp.concatenate([b_igate, b_fgate])
    bgc = jnp.pad(bias, (0, LANES - 2 * hn)).reshape(1, LANES)
    bgr = bias.reshape(2 * hn, 1)
    full = lambda *shape: pl.BlockSpec(shape, lambda b, i: (0,) * len(shape))
    return pl.pallas_call(
        functools.partial(_l0_in_kernel, tm=tm, width=width),
        grid=(bsz, s // tm),
        in_specs=[pl.BlockSpec((1, tm, d), lambda b, i: (b, i, 0)),
                  pl.BlockSpec((1, 6, d), lambda b, i: (b, 0, 0)),
                  full(1, d), full(d, 4 * width), full(d, 3 * width), full(d, LANES), full(2 * hn, d),
                  full(1, LANES), full(2 * hn, 1), full(3, width)],
        out_specs=[pl.BlockSpec((1, tm, 4 * width), lambda b, i: (b, i, 0)),
                   pl.BlockSpec((1, tm, width), lambda b, i: (b, i, 0)),
                   pl.BlockSpec((1, tm, LANES), lambda b, i: (b, i, 0)),
                   pl.BlockSpec((1, 2 * hn, tm), lambda b, i: (b, 0, i))],
        out_shape=[jax.ShapeDtypeStruct((bsz, s, 4 * width), BF16),
                   jax.ShapeDtypeStruct((bsz, s, width), BF16),
                   jax.ShapeDtypeStruct((bsz, s, LANES), F32),
                   jax.ShapeDtypeStruct((bsz, 2 * hn, s), F32)],
        scratch_shapes=[pltpu.VMEM((SUBLANES, width), F32)],
        compiler_params=_params("arbitrary", "arbitrary"),
        name="l0_in",
    )(x, mod, norm_g.reshape(1, d), wa, wc, wg, wgt, bgc, bgr, conv_w)


def _mlstm_kernel(qkvo_ref, gc_ref, gr_ref, nrm_ref, hm_ref, s_ref, n_ref, m_ref, *, hn, dh, chunk, nchunk):
    @pl.when(pl.program_id(1) == 0)
    def _():
        s_ref[...] = jnp.zeros_like(s_ref)
        n_ref[...] = jnp.zeros_like(n_ref)
        m_ref[...] = jnp.zeros_like(m_ref)

    width = hn * dh
    scale = dh ** -0.5
    ri = lax.broadcasted_iota(jnp.int32, (chunk, chunk), 0)
    ci = lax.broadcasted_iota(jnp.int32, (chunk, chunk), 1)
    causal = ci <= ri
    for c in range(nchunk):
        r0 = c * chunk
        gcol = gc_ref[0, r0:r0 + chunk, :]
        grow = gr_ref[0, :, r0:r0 + chunk]
        for h in range(hn):
            q = qkvo_ref[0, r0:r0 + chunk, h * dh:(h + 1) * dh]
            k = qkvo_ref[0, r0:r0 + chunk, width + h * dh:width + (h + 1) * dh]
            v = qkvo_ref[0, r0:r0 + chunk, 2 * width + h * dh:2 * width + (h + 1) * dh]
            o = qkvo_ref[0, r0:r0 + chunk, 3 * width + h * dh:3 * width + (h + 1) * dh]
            li_col = gcol[:, h:h + 1]
            lf_col = _log_sigmoid(gcol[:, hn + h:hn + h + 1])
            li_row = grow[h:h + 1, :]
            lf_row = _log_sigmoid(grow[hn + h:hn + h + 1, :])
            b_col = jnp.sum(jnp.where(causal, lf_row, 0.0), axis=1, keepdims=True)
            b_row = jnp.sum(jnp.where(ri <= ci, lf_col, 0.0), axis=0, keepdims=True)
            g = b_col[chunk - 1:chunk, :]
            a_col = g - b_col + li_col
            s_prev = s_ref[h]
            n_prev = n_ref[h]
            m_prev = m_ref[h]
            log_d = jnp.where(causal, b_col - b_row + li_row, -jnp.inf)
            inter_log = b_col + m_prev
            m_t = jnp.maximum(inter_log, jnp.max(log_d, axis=1, keepdims=True))
            qk = lax.dot_general(q, k, NT_DIMS, preferred_element_type=F32) * scale * jnp.exp(log_d - m_t)
            inter_w = jnp.exp(inter_log - m_t) * scale
            num = (jnp.dot(qk.astype(BF16), v, preferred_element_type=F32)
                   + inter_w * jnp.dot(q, s_prev.astype(BF16), preferred_element_type=F32))
            den = (jnp.sum(qk, axis=1, keepdims=True)
                   + inter_w * jnp.sum(q.astype(F32) * n_prev, axis=1, keepdims=True))
            hv = num / jnp.maximum(jnp.abs(den), jnp.exp(-m_t))
            hv = hv * lax.rsqrt(jnp.mean(hv * hv, axis=-1, keepdims=True) + EPS) * nrm_ref[:, h * dh:(h + 1) * dh]
            hm_ref[0, r0:r0 + chunk, h * dh:(h + 1) * dh] = (hv * jax.nn.sigmoid(o.astype(F32))).astype(BF16)
            m_new = jnp.maximum(g + m_prev, jnp.max(a_col, axis=0, keepdims=True))
            decay = jnp.exp(g + m_prev - m_new)
            kw = k.astype(F32) * jnp.exp(a_col - m_new)
            s_ref[h] = decay * s_prev + lax.dot_general(kw.astype(BF16), v, (((0,), (0,)), ((), ())),
                                                        preferred_element_type=F32)
            n_ref[h] = decay * n_prev + jnp.sum(kw, axis=0, keepdims=True)
            m_ref[h] = m_new


def _mlstm(qkvo, gates_col, gates_row, mlstm_norm, tq=512):
    bsz, s, w4 = qkvo.shape
    hn = MLSTM_HEADS
    width = w4 // 4
    dh = width // hn
    return pl.pallas_call(
        functools.partial(_mlstm_kernel, hn=hn, dh=dh, chunk=MLSTM_CHUNK, nchunk=tq // MLSTM_CHUNK),
        grid=(bsz, s // tq),
        in_specs=[pl.BlockSpec((1, tq, w4), lambda b, i: (b, i, 0)),
                  pl.BlockSpec((1, tq, LANES), lambda b, i: (b, i, 0)),
                  pl.BlockSpec((1, 2 * hn, tq), lambda b, i: (b, 0, i)),
                  pl.BlockSpec((1, width), lambda b, i: (0, 0))],
        out_specs=pl.BlockSpec((1, tq, width), lambda b, i: (b, i, 0)),
        out_shape=jax.ShapeDtypeStruct((bsz, s, width), BF16),
        scratch_shapes=[pltpu.VMEM((hn, dh, dh), F32), pltpu.VMEM((hn, 1, dh), F32), pltpu.VMEM((hn, 1, 1), F32)],
        compiler_params=_params("arbitrary", "arbitrary"),
        name="mlstm",
    )(qkvo, gates_col, gates_row, mlstm_norm.reshape(1, width))


def _projres_kernel(*refs, n_act, gate_row):
    acts, ws = refs[:n_act], refs[n_act:2 * n_act]
    x_ref, mod_ref, o_ref = refs[2 * n_act:]
    y = jnp.dot(acts[0][0], ws[0][...], preferred_element_type=F32)
    for a_ref, w_ref in zip(acts[1:], ws[1:]):
        y = y + jnp.dot(a_ref[0], w_ref[...], preferred_element_type=F32)
    gate = mod_ref[0][gate_row:gate_row + 1]
    o_ref[0] = x_ref[0] + (1.0 + gate) * y


def _projres(acts, ws, x, mod, gate_row, tm=512):
    bsz, s, d = x.shape
    n_act = len(acts)
    in_specs = ([pl.BlockSpec((1, tm, a.shape[2]), lambda b, i: (b, i, 0)) for a in acts]
                + [pl.BlockSpec(w.shape, lambda b, i: (0, 0)) for w in ws]
                + [pl.BlockSpec((1, tm, d), lambda b, i: (b, i, 0)),
                   pl.BlockSpec((1, 6, d), lambda b, i: (b, 0, 0))])
    return pl.pallas_call(
        functools.partial(_projres_kernel, n_act=n_act, gate_row=gate_row),
        grid=(bsz, s // tm),
        in_specs=in_specs,
        out_specs=pl.BlockSpec((1, tm, d), lambda b, i: (b, i, 0)),
        out_shape=jax.ShapeDtypeStruct((bsz, s, d), F32),
        compiler_params=_params("arbitrary", "arbitrary"),
        name="projres",
    )(*acts, *ws, x, mod)


def _l1_qkv_kernel(x_ref, mod_ref, g_ref, w_ref, qn_ref, kn_ref, q_ref, k_ref, v_ref, *, d, dh):
    mod = mod_ref[0]
    h = _modulate(x_ref[0], g_ref[...], mod[0:1], mod[1:2]).astype(BF16)
    y = jnp.dot(h, w_ref[...], preferred_element_type=F32)
    tm = y.shape[0]
    low = lax.broadcasted_iota(jnp.int32, (tm, LANES), 1) < dh
    scale = dh ** -0.5

    def headnorm(t, gain):
        sq = t * t
        s_low = jnp.sum(jnp.where(low, sq, 0.0), axis=-1, keepdims=True)
        s_all = jnp.sum(sq, axis=-1, keepdims=True)
        ms = jnp.where(low, s_low, s_all - s_low) * (1.0 / dh)
        return t * lax.rsqrt(ms + EPS) * gain

    for p in range(d // LANES):
        sl = slice(p * LANES, (p + 1) * LANES)
        q_ref[0, :, sl] = (headnorm(y[:, sl], qn_ref[...]) * scale).astype(BF16)
        k_ref[0, :, sl] = headnorm(y[:, d + p * LANES:d + (p + 1) * LANES], kn_ref[...]).astype(BF16)
    v_ref[0] = y[:, 2 * d:].astype(BF16)


def _l1_qkv(x, mod, norm_g, w_qkv, q_norm, k_norm, tm=512):
    bsz, s, d = x.shape
    dh = q_norm.shape[0]
    assert LANES == 2 * dh
    full = lambda *shape: pl.BlockSpec(shape, lambda b, i: (0,) * len(shape))
    tok = pl.BlockSpec((1, tm, d), lambda b, i: (b, i, 0))
    return pl.pallas_call(
        functools.partial(_l1_qkv_kernel, d=d, dh=dh),
        grid=(bsz, s // tm),
        in_specs=[tok, pl.BlockSpec((1, 6, d), lambda b, i: (b, 0, 0)), full(1, d), full(d, 3 * d),
                  full(1, LANES), full(1, LANES)],
        out_specs=[tok, tok, tok],
        out_shape=[jax.ShapeDtypeStruct((bsz, s, d), BF16)] * 3,
        compiler_params=_params("arbitrary", "arbitrary"),
        name="l1_qkv",
    )(x, mod, norm_g.reshape(1, d), w_qkv.astype(BF16), jnp.tile(q_norm, 2).reshape(1, LANES),
      jnp.tile(k_norm, 2).reshape(1, LANES))


def _sb_kernel(q_ref, k_ref, v_ref, o_ref, carry_ref, acc_ref, *, tq, dh):
    qi = pl.program_id(2)
    q = q_ref[0]
    lane = lax.broadcasted_iota(jnp.int32, (tq, LANES), 1)
    ri = lax.broadcasted_iota(jnp.int32, (tq, tq), 0)
    ci = lax.broadcasted_iota(jnp.int32, (tq, tq), 1)
    strict = ci < ri
    rhs = jnp.concatenate([jnp.where(ri > ci, 1.0, 0.0), jnp.ones((tq, LANES), F32)], axis=1).astype(BF16)
    rep = tq // LANES
    n_heads = LANES // dh
    hmasks = [(lane >= hh * dh) & (lane < (hh + 1) * dh) for hh in range(n_heads)]
    qms = [jnp.where(m, q, jnp.zeros_like(q)) for m in hmasks]

    def block(kb, diag):
        start = pl.multiple_of(kb * tq, tq)
        kblk = k_ref[0, pl.ds(start, tq), :]
        vblk = v_ref[0, pl.ds(start, tq), :]
        worst = None
        for hh in range(n_heads):
            z = lax.dot_general(qms[hh], kblk, NT_DIMS, preferred_element_type=F32)
            lsn = _log_sigmoid(-z)
            if diag:
                lsn = jnp.where(strict, lsn, 0.0)
            hi = lsn.astype(BF16)
            lo = (lsn - hi.astype(F32)).astype(BF16)
            rr = jnp.dot(hi, rhs, preferred_element_type=F32) + jnp.dot(lo, rhs, preferred_element_type=F32)
            logit = z + lsn + rr[:, :tq]
            if not diag:
                logit = logit + jnp.concatenate([carry_ref[hh]] * rep, axis=1)
            p = jnp.exp(logit)
            if diag:
                p = jnp.where(strict, p, 0.0)
            pv = jnp.dot(p.astype(BF16), vblk, preferred_element_type=F32)
            if diag:
                acc_ref[hh] = pv
                carry = rr[:, tq:]
            else:
                acc_ref[hh] += pv
                carry = carry_ref[hh] + rr[:, tq:]
            carry_ref[hh] = carry
            top = jnp.max(carry)
            worst = top if worst is None else jnp.maximum(worst, top)
        return worst

    def cond(state):
        it, worst = state
        return (it < qi) & (worst > -ATT_LOG_CUTOFF)

    def body(state):
        it, _ = state
        return it + 1, block(qi - 1 - it, False)

    lax.while_loop(cond, body, (jnp.int32(0), block(qi, True)))
    out = acc_ref[0]
    for hh in range(1, n_heads):
        out = jnp.where(hmasks[hh], acc_ref[hh], out)
    o_ref[0] = out.astype(BF16)


def _sb_attention(q, k, v, tq=256):
    bsz, s, d = q.shape
    dh = d // SB_HEADS
    npair = d // LANES
    return pl.pallas_call(
        functools.partial(_sb_kernel, tq=tq, dh=dh),
        grid=(bsz, npair, s // tq),
        in_specs=[pl.BlockSpec((1, tq, LANES), lambda b, p, i: (b, i, p)),
                  pl.BlockSpec((1, s, LANES), lambda b, p, i: (b, 0, p)),
                  pl.BlockSpec((1, s, LANES), lambda b, p, i: (b, 0, p))],
        out_specs=pl.BlockSpec((1, tq, LANES), lambda b, p, i: (b, i, p)),
        out_shape=jax.ShapeDtypeStruct((bsz, s, d), BF16),
        scratch_shapes=[pltpu.VMEM((LANES // dh, tq, LANES), F32), pltpu.VMEM((LANES // dh, tq, LANES), F32)],
        compiler_params=_params("arbitrary", "arbitrary", "arbitrary"),
        name="sb_attn",
    )(q, k, v)


def _sorter_pairs(lo, hi):
    def merge(lo, hi, r):
        step = r * 2
        if step < hi - lo:
            yield from merge(lo, hi, step)
            yield from merge(lo + r, hi, step)
            yield from ((i, i + r) for i in range(lo + r, hi - r, step))
        else:
            yield (lo, lo + r)

    if hi - lo >= 1:
        mid = lo + (hi - lo) // 2
        yield from _sorter_pairs(lo, mid)
        yield from _sorter_pairs(mid + 1, hi)
        yield from merge(lo, hi, 1)


def _exchange(x, i, j):
    x[i], x[j] = jnp.maximum(x[i], x[j]), jnp.minimum(x[i], x[j])


def _sort_bitonic(x):
    d = len(x) // 2
    while d:
        for i in range(len(x)):
            if not i & d:
                _exchange(x, i, i + d)
        d //= 2


def _top_values(s_ref, vals_ref):
    n, width = s_ref.shape
    n_tile = n // SUBLANES
    ok_total = 0.0
    for c0 in range(0, width, LANES):
        cols = slice(c0, c0 + LANES)
        rows = [s_ref[t * SUBLANES:(t + 1) * SUBLANES, cols] for t in range(n_tile)]
        x = list(rows)
        for i, j in _sorter_pairs(0, n_tile - 1):
            _exchange(x, i, j)
        for shift in (SUBLANES // 2, SUBLANES // 4, SUBLANES // 8):
            other = [pltpu.roll(v, shift, 0) for v in x]
            if len(x) < PEER_TOPK:
                x = x + other[::-1]
            else:
                x = [jnp.maximum(a, b) for a, b in zip(x, other[::-1])]
            _sort_bitonic(x)
        for i in range(PEER_TOPK):
            vals_ref[i:i + 1, cols] = x[i][0:1]
        gap = x[0] - x[1]
        for i in range(1, PEER_TOPK - 1):
            gap = jnp.minimum(gap, x[i] - x[i + 1])
        reach = jnp.zeros((SUBLANES, LANES), F32)
        for r in rows:
            reach = reach + jnp.where(r >= x[PEER_TOPK - 1], 1.0, 0.0)
        reach = jnp.sum(reach, axis=0, keepdims=True)
        ok = jnp.where((gap[0:1] > 0.0) & (reach == float(PEER_TOPK)), 1.0, 0.0)
        ok_total = ok_total + jnp.sum(ok)
    return ok_total == float(width)


def _rank_top(s_ref, rank_ref, vals_ref):
    n, width = s_ref.shape
    rowf = lax.broadcasted_iota(jnp.int32, (n, LANES), 0).astype(F32)
    slot = lax.broadcasted_iota(jnp.int32, (PEER_TOPK, LANES), 0)

    def body(i, carry):
        cur, rank, vals = carry
        mx = jnp.max(cur, axis=0, keepdims=True)
        hit = rowf == jnp.min(jnp.where(cur == mx, rowf, float(n)), axis=0, keepdims=True)
        return (jnp.where(hit, -jnp.inf, cur), jnp.where(hit, jnp.asarray(i, F32), rank),
                jnp.where(slot == i, mx, vals))

    for c0 in range(0, width, LANES):
        cols = slice(c0, c0 + LANES)
        init = (s_ref[:, cols], jnp.full((n, LANES), float(PEER_TOPK), F32), jnp.zeros((PEER_TOPK, LANES), F32))
        _, rank, vals = lax.fori_loop(0, PEER_TOPK, body, init)
        rank_ref[:, cols] = rank
        vals_ref[:, cols] = vals


def _twice_bf16(x):
    bits = pltpu.bitcast(x.astype(BF16).astype(F32), jnp.uint32)
    return bits | (bits >> 16)


def _peer_query_kernel(*refs, heads, nk, n_act):
    acts, ws = refs[:n_act], refs[n_act:2 * n_act]
    (x_ref, mod_ref, g_ref, wq_ref, k1_ref, k2_ref,
     x1_ref, h_ref, a1_ref, c1_ref, b1_ref, r2_ref,
     q_ref, s_ref, r_ref, v_ref, cand_ref, crank_ref, cval_ref) = refs[2 * n_act:]
    topk = PEER_TOPK
    mod = mod_ref[0]
    y = jnp.dot(acts[0][0], ws[0][...], preferred_element_type=F32)
    for a_ref, w_ref in zip(acts[1:], ws[1:]):
        y = y + jnp.dot(a_ref[0], w_ref[...], preferred_element_type=F32)
    x1 = x_ref[0] + (1.0 + mod[2:3]) * y
    x1_ref[0] = x1
    h = _modulate(x1, g_ref[...], mod[3:4], mod[4:5]).astype(BF16)
    h_ref[0] = h
    q_ref[...] = jnp.dot(h, wq_ref[...], preferred_element_type=F32).astype(BF16)
    tp = q_ref.shape[0]
    pairs = [(i, topk // (i + 1)) for i in range(topk)]
    ncand = sum(n for _, n in pairs)
    ncand_pad = cand_ref.shape[0]

    for hd in range(heads):
        for half, keys_ref in enumerate((k1_ref, k2_ref)):
            qh = q_ref[:, (2 * hd + half) * nk:(2 * hd + half + 1) * nk]
            s_ref[2 * hd + half] = lax.dot_general(keys_ref[...], qh, NT_DIMS, preferred_element_type=F32)

    def build_candidates():
        off = 0
        for i, n in pairs:
            cand_ref[off:off + n, :] = v_ref[0, i:i + 1, :] + v_ref[1, 0:n, :]
            off += n
        cand_ref[ncand:ncand_pad, :] = jnp.full((ncand_pad - ncand, tp), -jnp.inf, F32)

    def emit(hd, a1, cnt1, b1, r2):
        a1_ref[hd] = _twice_bf16(a1)
        c1_ref[hd] = _twice_bf16(cnt1)
        b1w = pltpu.bitcast(b1.astype(BF16), jnp.uint32)
        r2w = pltpu.bitcast(r2.astype(BF16), jnp.uint32)
        for lt in range(tp // LANES):
            b1_ref[lt, hd] = b1w[:, lt * LANES:(lt + 1) * LANES]
            r2_ref[lt, hd] = r2w[:, lt * LANES:(lt + 1) * LANES]

    def softmax_norm(picked):
        cand = cand_ref[...]
        return jnp.sum(picked * jnp.exp(cand - cand[0:1]), axis=0, keepdims=True)

    def pair_counts(picked):
        counts, off = [], 0
        for _, n in pairs:
            counts.append(jnp.sum(picked[off:off + n], axis=0, keepdims=True))
            off += n
        return counts

    for hd in range(heads):
        s1_ref, s2_ref = s_ref.at[2 * hd], s_ref.at[2 * hd + 1]
        distinct = _top_values(s1_ref, v_ref.at[0]) & _top_values(s2_ref, v_ref.at[1])
        build_candidates()
        distinct = distinct & _top_values(cand_ref, cval_ref)
        picked = jnp.where(cand_ref[...] >= cval_ref[topk - 1:topk, :], 1.0, 0.0)
        half_over_z = 0.5 / softmax_norm(picked)
        counts = pair_counts(picked) + [jnp.zeros((1, tp), F32)]
        for lt in range(tp // LANES):
            cols = slice(lt * LANES, (lt + 1) * LANES)
            s1 = s1_ref[:, cols]
            cnt1 = jnp.broadcast_to(counts[0][:, cols], (nk, LANES))
            for i in range(topk):
                cnt1 = jnp.where(s1 < v_ref[0, i:i + 1, cols], counts[i + 1][:, cols], cnt1)
            a1 = jnp.where(s1 >= v_ref[0, topk - 1:topk, cols], jnp.exp(s1 - v_ref[0, 0:1, cols]), 0.0) * half_over_z[:, cols]
            a1_ref[hd, :, cols] = _twice_bf16(a1)
            c1_ref[hd, :, cols] = _twice_bf16(cnt1)
            s2 = s2_ref[:, cols]
            r2 = jnp.zeros((nk, LANES), F32)
            for i in range(topk):
                r2 = jnp.where(s2 < v_ref[1, i:i + 1, cols], float(i + 1), r2)
            b1 = jnp.where(s2 >= v_ref[1, topk - 1:topk, cols], jnp.exp(s2 - v_ref[1, 0:1, cols]), 0.0)
            b1_ref[lt, hd] = pltpu.bitcast(b1.astype(BF16), jnp.uint32)
            r2_ref[lt, hd] = pltpu.bitcast(r2.astype(BF16), jnp.uint32)

        @pl.when(jnp.logical_not(distinct))
        def _():
            _rank_top(s1_ref, r_ref.at[0], v_ref.at[0])
            _rank_top(s2_ref, r_ref.at[1], v_ref.at[1])
            build_candidates()
            _rank_top(cand_ref, crank_ref, cval_ref)
            picked = jnp.where(crank_ref[...] < topk, 1.0, 0.0)
            z = softmax_norm(picked)
            r1, r2 = r_ref[0], r_ref[1]
            cnt1 = jnp.zeros((nk, tp), F32)
            for i, c_i in enumerate(pair_counts(picked)):
                cnt1 = jnp.where(r1 == float(i), c_i, cnt1)
            a1 = jnp.where(r1 < topk, jnp.exp(s1_ref[...] - v_ref[0, 0:1, :]), 0.0) * (0.5 / z)
            b1 = jnp.where(r2 < topk, jnp.exp(s2_ref[...] - v_ref[1, 0:1, :]), 0.0)
            emit(hd, a1, cnt1, b1, r2)


def _peer_query(acts, ws, x, mod, norm_g, wq, k1, k2, tp=256):
    bsz, s, d = x.shape
    n_act = len(acts)
    nk = k1.shape[0]
    heads = wq.shape[1] // (2 * k1.shape[1])
    assert heads == PEER_HEADS and k1.shape[1] == nk
    ntok = bsz * s
    nt = s // tp
    full = lambda *shape: pl.BlockSpec(shape, lambda b, i: (0,) * len(shape))
    gate_spec = pl.BlockSpec((heads, nk, tp), lambda b, i: (0, 0, b * nt + i))
    gate_shape = jax.ShapeDtypeStruct((heads, nk, ntok), jnp.uint32)
    pair_spec = pl.BlockSpec((tp // LANES, heads, nk // 2, LANES), lambda b, i: (b * nt + i, 0, 0, 0))
    pair_shape = jax.ShapeDtypeStruct((ntok // LANES, heads, nk // 2, LANES), jnp.uint32)
    ncand_pad = 64
    return pl.pallas_call(
        functools.partial(_peer_query_kernel, heads=heads, nk=nk, n_act=n_act),
        grid=(bsz, nt),
        in_specs=([pl.BlockSpec((1, tp, a.shape[2]), lambda b, i: (b, i, 0)) for a in acts]
                  + [full(*w.shape) for w in ws]
                  + [pl.BlockSpec((1, tp, d), lambda b, i: (b, i, 0)),
                     pl.BlockSpec((1, 6, d), lambda b, i: (b, 0, 0)),
                     full(1, d), full(d, wq.shape[1]), full(nk, nk), full(nk, nk)]),
        out_specs=[pl.BlockSpec((1, tp, d), lambda b, i: (b, i, 0)), pl.BlockSpec((1, tp, d), lambda b, i: (b, i, 0)),
                   gate_spec, gate_spec, pair_spec, pair_spec],
        out_shape=[jax.ShapeDtypeStruct((bsz, s, d), F32), jax.ShapeDtypeStruct((bsz, s, d), BF16),
                   gate_shape, gate_shape, pair_shape, pair_shape],
        scratch_shapes=[pltpu.VMEM((tp, wq.shape[1]), BF16), pltpu.VMEM((2 * heads, nk, tp), F32),
                        pltpu.VMEM((2, nk, tp), F32), pltpu.VMEM((2, PEER_TOPK, tp), F32),
                        pltpu.VMEM((ncand_pad, tp), F32), pltpu.VMEM((ncand_pad, tp), F32),
                        pltpu.VMEM((PEER_TOPK, tp), F32)],
        compiler_params=_params("arbitrary", "arbitrary"),
        name="peer_query",
    )(*acts, *ws, x, mod, norm_g.reshape(1, d), wq.astype(BF16), k1.astype(BF16), k2.astype(BF16))


def _peer_expert_kernel(h_ref, u_ref, vt_ref, a1_ref, c1_ref, b1_ref, r2_ref, x_ref, mod_ref, o_ref,
                        s_ref, w_ref, acc_ref, *, heads, nk, n_e1):
    j = pl.program_id(1)

    @pl.when(j == 0)
    def _():
        acc_ref[...] = jnp.zeros_like(acc_ref)

    tt = h_ref.shape[0]
    zero = jnp.zeros((nk, LANES), BF16)

    def bcast_rows(ref, hd, e, cols):
        return pltpu.bitcast(jnp.broadcast_to(ref[hd, e:e + 1, cols], (nk // 2, LANES)), BF16)

    s = lax.dot_general(u_ref[...], h_ref[...], NT_DIMS, preferred_element_type=F32)
    for lt in range(tt // LANES):
        s_ref[lt] = s[:, lt * LANES:(lt + 1) * LANES]

    def per_lane_tile(lt, _):
        cols = pl.ds(pl.multiple_of(lt * LANES, LANES), LANES)
        for e in range(n_e1):
            rows = slice(e * nk, (e + 1) * nk)
            s = s_ref[lt, rows, :].astype(BF16)
            act = s * (1.0 + lax.erf(s * (1.0 / math.sqrt(2.0))))
            gate = zero
            for hd in range(heads):
                cnt = bcast_rows(c1_ref, hd, e, cols)
                a1 = bcast_rows(a1_ref, hd, e, cols)
                r2 = pltpu.bitcast(r2_ref[lt, hd], BF16)
                b1 = pltpu.bitcast(b1_ref[lt, hd], BF16)
                gate = gate + jnp.where(r2 < cnt, b1, zero) * a1
            w_ref[rows, cols] = act * gate
        return 0

    lax.fori_loop(0, tt // LANES, per_lane_tile, 0)
    acc_ref[...] += jnp.dot(vt_ref[...], w_ref[...], preferred_element_type=F32)

    @pl.when(j == pl.num_programs(1) - 1)
    def _():
        o_ref[...] = x_ref[...] + (1.0 + mod_ref[0][5:6]) * acc_ref[...].T


def _peer_experts(h2, a1, c1, b1, r2, x, mod, expert_u, expert_v, tt=1024, ec=1024):
    bsz, s, d = x.shape
    ntok = bsz * s
    heads, nk, _ = a1.shape
    n_exp = expert_u.shape[0]
    n_e1 = ec // nk
    tt = min(tt, s)
    tiles_per_seq = s // tt
    u = expert_u.astype(BF16)
    vt = expert_v.T.astype(BF16)
    out = pl.pallas_call(
        functools.partial(_peer_expert_kernel, heads=heads, nk=nk, n_e1=n_e1),
        grid=(ntok // tt, n_exp // ec),
        in_specs=[pl.BlockSpec((tt, d), lambda i, j: (i, 0)),
                  pl.BlockSpec((ec, d), lambda i, j: (j, 0)),
                  pl.BlockSpec((d, ec), lambda i, j: (0, j)),
                  pl.BlockSpec((heads, n_e1, tt), lambda i, j: (0, j, i)),
                  pl.BlockSpec((heads, n_e1, tt), lambda i, j: (0, j, i)),
                  pl.BlockSpec((tt // LANES, heads, nk // 2, LANES), lambda i, j: (i, 0, 0, 0)),
                  pl.BlockSpec((tt // LANES, heads, nk // 2, LANES), lambda i, j: (i, 0, 0, 0)),
                  pl.BlockSpec((tt, d), lambda i, j: (i, 0)),
                  pl.BlockSpec((1, 6, d), lambda i, j: (i // tiles_per_seq, 0, 0))],
        out_specs=pl.BlockSpec((tt, d), lambda i, j: (i, 0)),
        out_shape=jax.ShapeDtypeStruct((ntok, d), F32),
        scratch_shapes=[pltpu.VMEM((tt // LANES, ec, LANES), F32), pltpu.VMEM((ec, tt), BF16),
                        pltpu.VMEM((d, tt), F32)],
        compiler_params=_params("arbitrary", "arbitrary"),
        name="peer_experts",
    )(h2.reshape(ntok, d), u, vt, a1, c1, b1, r2, x.reshape(ntok, d), mod)
    return out.reshape(bsz, s, d)


def _mix_peer(acts, ws, x, mod, norm_g, wq, k1, k2, expert_u, expert_v):
    x1, h2, a1, c1, b1, r2 = _peer_query(acts, ws, x, mod, norm_g, wq, k1, k2)
    return _peer_experts(h2, a1, c1, b1, r2, x1, mod, expert_u, expert_v)


def kernel(x, c, l0_ada_w, l0_ada_b, l0_norm_mix, l0_w_in, l0_b_igate, l0_b_fgate, l0_conv_w, l0_mlstm_norm, l0_w_out, l0_norm_ffn, l0_peer_wq, l0_peer_k1, l0_peer_k2, l0_peer_u, l0_peer_v, l1_ada_w, l1_ada_b, l1_norm_mix, l1_w_qkv, l1_q_norm, l1_k_norm, l1_w_out, l1_norm_ffn, l1_peer_wq, l1_peer_k1, l1_peer_k2, l1_peer_u, l1_peer_v):
    mod0 = _ada(c, l0_ada_w, l0_ada_b)
    qkvo, hc, gates_col, gates_row = _l0_in(x, mod0, l0_norm_mix, l0_w_in, l0_b_igate, l0_b_fgate, l0_conv_w)
    hm = _mlstm(qkvo, gates_col, gates_row, l0_mlstm_norm)
    width = hm.shape[2]
    w_out0 = l0_w_out.astype(BF16)
    x = _mix_peer([hm, hc], [w_out0[:width], w_out0[width:]], x, mod0, l0_norm_ffn,
                  l0_peer_wq, l0_peer_k1, l0_peer_k2, l0_peer_u, l0_peer_v)
    mod1 = _ada(c, l1_ada_w, l1_ada_b)
    q, k, v = _l1_qkv(x, mod1, l1_norm_mix, l1_w_qkv, l1_q_norm, l1_k_norm)
    o = _sb_attention(q, k, v)
    x = _mix_peer([o], [l1_w_out.astype(BF16)], x, mod1, l1_norm_ffn,
                  l1_peer_wq, l1_peer_k1, l1_peer_k2, l1_peer_u, l1_peer_v)
    return x
```
